```python
import math
import jax
import jax.numpy as jnp
from jax import lax
import numpy as np


D_MODEL = 2048
BATCH = 2
SEQ = 16384
DEPTH = 2

N_A_LAYERS = DEPTH // 2
N_B_LAYERS = DEPTH - N_A_LAYERS
SSM_GROUP = 16
SSM_GROUPS = D_MODEL // SSM_GROUP
SSM_STATE = 64
SCAN_CHUNK = 128
HEAD_DIM = 128
N_HEADS = D_MODEL // HEAD_DIM
DILATION_CFG = ((128, 1), (512, 4), (2048, 16))
N_GROUPS = len(DILATION_CFG)
ATTN_BLOCK = 128
D_FF = -(-8 * D_MODEL // (3 * 256)) * 256
EPS = 1e-6

kernel_name = 'yoco_s5_dilated_window_hybrid'


def rmsnorm(x, g):
    xf = x.astype(jnp.float32)
    y = xf * lax.rsqrt(jnp.mean(xf * xf, axis=-1, keepdims=True) + EPS) * g.astype(jnp.float32)
    return y.astype(x.dtype)


def swiglu_ffn(h, w_gate_up, w_down):
    gate, up = jnp.split(h @ w_gate_up, 2, axis=-1)
    return (jax.nn.silu(gate) * up) @ w_down


def _complex_scan_combine(e1, e2):
    a1r, a1i, b1r, b1i = e1
    a2r, a2i, b2r, b2i = e2
    ar = a2r * a1r - a2i * a1i
    ai = a2r * a1i + a2i * a1r
    br = a2r * b1r - a2i * b1i + b2r
    bi = a2r * b1i + a2i * b1r + b2i
    return (ar, ai, br, bi)


def s5_mixer(u, a_re, a_im, log_dt, b_re, b_im, c_re, c_im, d_skip, w_glu):
    bsz, seqlen, _ = u.shape
    f32 = jnp.float32
    lam_re = jnp.minimum(a_re.astype(f32), -1e-4)
    lam_im = a_im.astype(f32)
    dt = jnp.exp(log_dt.astype(f32))[:, None]
    mag = jnp.exp(lam_re * dt)
    ang = lam_im * dt
    lb_re = mag * jnp.cos(ang)
    lb_im = mag * jnp.sin(ang)
    den = lam_re * lam_re + lam_im * lam_im
    nr = lb_re - 1.0
    ni = lb_im
    coef_re = (nr * lam_re + ni * lam_im) / den
    coef_im = (ni * lam_re - nr * lam_im) / den
    br = b_re.astype(f32)
    bi = b_im.astype(f32)
    bb_re = coef_re[..., None] * br - coef_im[..., None] * bi
    bb_im = coef_re[..., None] * bi + coef_im[..., None] * br
    cr = c_re.astype(f32)
    ci = c_im.astype(f32)
    a_shape = (bsz, SCAN_CHUNK, SSM_GROUPS, SSM_STATE)
    a_el_re = jnp.broadcast_to(lb_re, a_shape)
    a_el_im = jnp.broadcast_to(lb_im, a_shape)

    uf = u.astype(f32)
    n_chunks = seqlen // SCAN_CHUNK
    u_chunks = uf.reshape(bsz, n_chunks, SCAN_CHUNK, SSM_GROUPS, SSM_GROUP).transpose(1, 0, 2, 3, 4)

    def step(carry, u_c):
        h_re, h_im = carry
        bu_re = jnp.einsum('blgh,gph->blgp', u_c, bb_re)
        bu_im = jnp.einsum('blgh,gph->blgp', u_c, bb_im)
        ar, ai, sr, si = lax.associative_scan(_complex_scan_combine, (a_el_re, a_el_im, bu_re, bu_im), axis=1)
        st_re = ar * h_re[:, None] - ai * h_im[:, None] + sr
        st_im = ar * h_im[:, None] + ai * h_re[:, None] + si
        y = jnp.einsum('blgp,ghp->blgh', st_re, cr) - jnp.einsum('blgp,ghp->blgh', st_im, ci)
        return (st_re[:, -1], st_im[:, -1]), y

    h0 = (jnp.zeros((bsz, SSM_GROUPS, SSM_STATE), f32), jnp.zeros((bsz, SSM_GROUPS, SSM_STATE), f32))
    _, y = lax.scan(step, h0, u_chunks)
    y = y.transpose(1, 0, 2, 3, 4).reshape(bsz, seqlen, D_MODEL) + d_skip.astype(f32) * uf
    g = jax.nn.gelu(y).astype(u.dtype)
    val, gate = jnp.split(g @ w_glu, 2, axis=-1)
    return (val.astype(f32) * jax.nn.sigmoid(gate.astype(f32))).astype(u.dtype)


def alibi_slopes():
    n = N_GROUPS * N_HEADS
    i = jnp.arange(1, n + 1, dtype=jnp.float32)
    s = jnp.exp2(-8.0 * i / n)
    return s.reshape(N_HEADS, N_GROUPS).T


def head_rmsnorm(t, g):
    tf = t.astype(jnp.float32)
    y = tf * lax.rsqrt(jnp.mean(tf * tf, axis=-1, keepdims=True) + EPS) * g.astype(jnp.float32)
    return y.astype(t.dtype)


def shared_kv(x, kv_norm, w_kv, k_norm):
    bsz, seqlen, _ = x.shape
    kv = (rmsnorm(x, kv_norm) @ w_kv).reshape(bsz, seqlen, 2, N_HEADS, HEAD_DIM)
    k = head_rmsnorm(kv[:, :, 0], k_norm)
    v = kv[:, :, 1]
    return k, v


def dilated_window_group(q, k, v, window, dilation, slopes):
    bsz, seqlen, nh, dh = q.shape
    span = dilation * ATTN_BLOCK
    lp = -(-seqlen // span) * span
    pad = lp - seqlen
    n_sub = lp // dilation
    nb = n_sub // ATTN_BLOCK
    w_sub = window // dilation

    def to_blocks(t):
        t = jnp.pad(t, ((0, 0), (0, pad), (0, 0), (0, 0)))
        t = t.reshape(bsz, n_sub, dilation, nh, dh).transpose(0, 2, 1, 3, 4)
        return t.reshape(bsz, dilation, nb, ATTN_BLOCK, nh, dh)

    def with_prev(t):
        prev = jnp.pad(t[:, :, :-1], ((0, 0), (0, 0), (1, 0), (0, 0), (0, 0), (0, 0)))
        return jnp.concatenate([prev, t], axis=3)

    qb = to_blocks(q)
    kk = with_prev(to_blocks(k))
    vv = with_prev(to_blocks(v))
    s = jnp.einsum('bdnqhe,bdnkhe->bdnhqk', qb, kk, preferred_element_type=jnp.float32) * (dh ** -0.5)
    qi = jnp.arange(ATTN_BLOCK)[:, None] + ATTN_BLOCK
    kj = jnp.arange(2 * ATTN_BLOCK)[None, :]
    dist = qi - kj
    valid = (dist >= 0) & (dist <= w_sub)
    valid = valid[None] & ((jnp.arange(nb)[:, None, None] > 0) | (kj >= ATTN_BLOCK)[None])
    bias = -slopes[:, None, None] * (dilation * dist).astype(jnp.float32)[None]
    s = jnp.where(valid[:, None], s + bias, -jnp.inf)
    lse = jax.nn.logsumexp(s, axis=-1)
    p = jnp.exp(s - lse[..., None])
    o = jnp.einsum('bdnhqk,bdnkhe->bdnqhe', p.astype(v.dtype), vv, preferred_element_type=jnp.float32)
    o = o.reshape(bsz, dilation, n_sub, nh, dh).transpose(0, 2, 1, 3, 4).reshape(bsz, lp, nh, dh)[:, :seqlen]
    lse = lse.transpose(0, 1, 2, 4, 3).reshape(bsz, dilation, n_sub, nh)
    lse = lse.transpose(0, 2, 1, 3).reshape(bsz, lp, nh)[:, :seqlen]
    return o, lse


def dilated_attention(h, k, v, q_norm_l, w_q_l, w_o_l):
    bsz, seqlen, _ = h.shape
    q = (h @ w_q_l).reshape(bsz, seqlen, N_GROUPS, N_HEADS, HEAD_DIM)
    q = head_rmsnorm(q, q_norm_l[:, None, :])
    slopes = alibi_slopes()
    outs = []
    lses = []
    for g, (win, dil) in enumerate(DILATION_CFG):
        o, lse = dilated_window_group(q[:, :, g], k, v, win, dil, slopes[g])
        outs.append(o)
        lses.append(lse)
    wts = jax.nn.softmax(jnp.stack(lses, axis=0), axis=0)
    merged = jnp.sum(wts[..., None] * jnp.stack(outs, axis=0), axis=0)
    return merged.reshape(bsz, seqlen, N_HEADS * HEAD_DIM).astype(h.dtype) @ w_o_l


def setup_inputs(seed: int = 0) -> dict:
    key = jax.random.key(seed)
    ks = jax.random.split(key, 24)
    f32 = jnp.float32
    d = D_MODEL
    qkv_w = N_HEADS * HEAD_DIM
    nrm = lambda k, shape, scale: jax.random.normal(k, shape, f32) * scale
    n_idx = jnp.arange(SSM_STATE, dtype=f32)
    x = jax.random.normal(ks[0], (BATCH, SEQ, d), f32)
    mix_norm = 1.0 + nrm(ks[1], (DEPTH, d), 0.02)
    ffn_norm = 1.0 + nrm(ks[2], (DEPTH, d), 0.02)
    ssm_a_re = -0.5 + nrm(ks[3], (N_A_LAYERS, SSM_GROUPS, SSM_STATE), 0.01)
    ssm_a_im = math.pi * n_idx + nrm(ks[4], (N_A_LAYERS, SSM_GROUPS, SSM_STATE), 0.01)
    ssm_log_dt = jax.random.uniform(ks[5], (N_A_LAYERS, SSM_GROUPS), f32, math.log(1e-3), math.log(1e-1))
    ssm_b_re = nrm(ks[6], (N_A_LAYERS, SSM_GROUPS, SSM_STATE, SSM_GROUP), SSM_GROUP ** -0.5)
    ssm_b_im = nrm(ks[7], (N_A_LAYERS, SSM_GROUPS, SSM_STATE, SSM_GROUP), SSM_GROUP ** -0.5)
    ssm_c_re = nrm(ks[8], (N_A_LAYERS, SSM_GROUPS, SSM_GROUP, SSM_STATE), SSM_STATE ** -0.5)
    ssm_c_im = nrm(ks[9], (N_A_LAYERS, SSM_GROUPS, SSM_GROUP, SSM_STATE), SSM_STATE ** -0.5)
    ssm_d = nrm(ks[10], (N_A_LAYERS, d), 1.0)
    w_glu = nrm(ks[11], (N_A_LAYERS, d, 2 * d), d ** -0.5)
    kv_norm = 1.0 + nrm(ks[12], (d,), 0.02)
    w_kv = nrm(ks[13], (d, 2 * qkv_w), d ** -0.5)
    k_norm = 1.0 + nrm(ks[14], (HEAD_DIM,), 0.02)
    w_q = nrm(ks[15], (N_B_LAYERS, d, N_GROUPS * qkv_w), d ** -0.5)
    q_norm = 1.0 + nrm(ks[16], (N_B_LAYERS, N_GROUPS, HEAD_DIM), 0.02)
    w_o = nrm(ks[17], (N_B_LAYERS, qkv_w, d), qkv_w ** -0.5)
    w_gate_up = nrm(ks[18], (DEPTH, d, 2 * D_FF), d ** -0.5)
    w_down = nrm(ks[19], (DEPTH, D_FF, d), D_FF ** -0.5)
    return {'x': x, 'mix_norm': mix_norm, 'ffn_norm': ffn_norm,
            'ssm_a_re': ssm_a_re, 'ssm_a_im': ssm_a_im, 'ssm_log_dt': ssm_log_dt,
            'ssm_b_re': ssm_b_re, 'ssm_b_im': ssm_b_im, 'ssm_c_re': ssm_c_re, 'ssm_c_im': ssm_c_im,
            'ssm_d': ssm_d, 'w_glu': w_glu, 'kv_norm': kv_norm, 'w_kv': w_kv, 'k_norm': k_norm,
            'w_q': w_q, 'q_norm': q_norm, 'w_o': w_o, 'w_gate_up': w_gate_up, 'w_down': w_down}


def reference(x, mix_norm, ffn_norm, ssm_a_re, ssm_a_im, ssm_log_dt, ssm_b_re, ssm_b_im, ssm_c_re, ssm_c_im,
              ssm_d, w_glu, kv_norm, w_kv, k_norm, w_q, q_norm, w_o, w_gate_up, w_down):
    k = None
    v = None
    for layer in range(DEPTH):
        h = rmsnorm(x, mix_norm[layer])
        if layer < N_A_LAYERS:
            i = layer
            x = x + s5_mixer(h, ssm_a_re[i], ssm_a_im[i], ssm_log_dt[i], ssm_b_re[i], ssm_b_im[i],
                             ssm_c_re[i], ssm_c_im[i], ssm_d[i], w_glu[i])
        else:
            if layer == N_A_LAYERS:
                k, v = shared_kv(x, kv_norm, w_kv, k_norm)
            j = layer - N_A_LAYERS
            x = x + dilated_attention(h, k, v, q_norm[j], w_q[j], w_o[j])
        x = x + swiglu_ffn(rmsnorm(x, ffn_norm[layer]), w_gate_up[layer], w_down[layer])
    return x
```

```python
import functools
import math

import jax
import jax.numpy as jnp
from jax import lax
from jax.experimental import pallas as pl
from jax.experimental.pallas import tpu as pltpu

F32 = jnp.float32
BF16 = jnp.bfloat16

EPS = 1e-6
SSM_GROUP = 16
SSM_STATE = 64
HEAD_DIM = 128
ATTN_BLOCK = 128
DILATION_CFG = ((128, 1), (512, 4), (2048, 16))
N_GROUPS = len(DILATION_CFG)
NEG_BIG = -1e30

V7X_SUBLANES = 8
V7X_LANES = 128
V7X_MXU_DIM = 256
V7X_VMEM_BYTES = 64 * 1024 * 1024
VMEM_LIMIT = V7X_VMEM_BYTES - 8 * 1024 * 1024


def _params(semantics):
    return pltpu.CompilerParams(dimension_semantics=semantics, vmem_limit_bytes=VMEM_LIMIT)


def _const_spec(shape):
    nd = len(shape)
    return pl.BlockSpec(shape, lambda *_: (0,) * nd, pipeline_mode=pl.Buffered(1))


def _rms_scale(x):
    return lax.rsqrt(jnp.mean(x * x, axis=-1, keepdims=True) + EPS)


S5_TILE_GROUPS = V7X_MXU_DIM // SSM_GROUP
S5_TILE_STATES = S5_TILE_GROUPS * SSM_STATE
S5_STEPS = (1, 2, 4)


def _s5_kernel(x_ref, gain_ref, bw_ref, cw_ref, tab_ref, dskip_ref, o_ref, h_ref, slab_ref, *, rows):
    ns = S5_TILE_STATES
    n_tiles = bw_ref.shape[0]

    @pl.when(pl.program_id(1) == 0)
    def _():
        h_ref[...] = jnp.zeros_like(h_ref)

    x = x_ref[...]
    u = x * _rms_scale(x) * gain_ref[...]
    u_bf = u.astype(BF16)

    for m in range(n_tiles):
        lo = m * V7X_MXU_DIM
        hi = lo + V7X_MXU_DIM
        slab_ref[...] = jnp.dot(u_bf[:, lo:hi], bw_ref[m], preferred_element_type=F32)

        def tile_scan(j, h, m=m):
            r0 = pl.multiple_of(j * V7X_SUBLANES, V7X_SUBLANES)
            xr = slab_ref[pl.ds(r0, V7X_SUBLANES), :ns]
            xi = slab_ref[pl.ds(r0, V7X_SUBLANES), ns:]
            for t, s in enumerate(S5_STEPS):
                ar = tab_ref[m, t, :, :ns]
                ai = tab_ref[m, t, :, ns:]
                rr = pltpu.roll(xr, s, 0)
                ri = pltpu.roll(xi, s, 0)
                xr, xi = xr + (ar * rr - ai * ri), xi + (ar * ri + ai * rr)
            pr = tab_ref[m, len(S5_STEPS), :, :ns]
            pi = tab_ref[m, len(S5_STEPS), :, ns:]
            hr = h[:, :ns]
            hi_ = h[:, ns:]
            xr = xr + (pr * hr - pi * hi_)
            xi = xi + (pr * hi_ + pi * hr)
            slab_ref[pl.ds(r0, V7X_SUBLANES), :ns] = xr
            slab_ref[pl.ds(r0, V7X_SUBLANES), ns:] = xi
            last = jnp.concatenate([xr[V7X_SUBLANES - 1:, :], xi[V7X_SUBLANES - 1:, :]], axis=1)
            return jnp.broadcast_to(last, (V7X_SUBLANES, 2 * ns))

        h_ref[m] = lax.fori_loop(0, rows // V7X_SUBLANES, tile_scan, h_ref[m])

        y = jnp.dot(slab_ref[...].astype(BF16), cw_ref[m], preferred_element_type=F32)
        y = y + dskip_ref[:, lo:hi] * u[:, lo:hi]
        o_ref[:, lo:hi] = jax.nn.gelu(y).astype(o_ref.dtype)


def _s5_prepare(a_re, a_im, log_dt, b_re, b_im, c_re, c_im):
    lam_re = jnp.minimum(a_re.astype(F32), -1e-4)
    lam_im = a_im.astype(F32)
    dt = jnp.exp(log_dt.astype(F32))[:, None]
    mag = jnp.exp(lam_re * dt)
    ang = lam_im * dt
    lb_re = mag * jnp.cos(ang)
    lb_im = mag * jnp.sin(ang)
    den = lam_re * lam_re + lam_im * lam_im
    nr = lb_re - 1.0
    ni = lb_im
    coef_re = (nr * lam_re + ni * lam_im) / den
    coef_im = (ni * lam_re - nr * lam_im) / den
    br = b_re.astype(F32)
    bi = b_im.astype(F32)
    bb_re = coef_re[..., None] * br - coef_im[..., None] * bi
    bb_im = coef_re[..., None] * bi + coef_im[..., None] * br

    n_groups = a_re.shape[0]
    n_tiles = n_groups // S5_TILE_GROUPS
    eye = jnp.eye(S5_TILE_GROUPS, dtype=F32)

    def pack_b(w):
        w = w.reshape(n_tiles, S5_TILE_GROUPS, SSM_STATE, SSM_GROUP)
        return jnp.einsum('mgph,gk->mghkp', w, eye).reshape(n_tiles, V7X_MXU_DIM, S5_TILE_STATES)

    def pack_c(w):
        w = w.reshape(n_tiles, S5_TILE_GROUPS, SSM_GROUP, SSM_STATE)
        return jnp.einsum('mghp,gk->mgpkh', w, eye).reshape(n_tiles, S5_TILE_STATES, V7X_MXU_DIM)

    bw = jnp.concatenate([pack_b(bb_re), pack_b(bb_im)], axis=-1).astype(BF16)
    cw = jnp.concatenate([pack_c(c_re.astype(F32)), -pack_c(c_im.astype(F32))], axis=1).astype(BF16)

    pw_re = [lb_re]
    pw_im = [lb_im]
    for _ in range(V7X_SUBLANES - 1):
        pr, pi = pw_re[-1], pw_im[-1]
        pw_re.append(pr * lb_re - pi * lb_im)
        pw_im.append(pr * lb_im + pi * lb_re)
    pw_re = jnp.stack(pw_re).reshape(V7X_SUBLANES, n_tiles, S5_TILE_STATES)
    pw_im = jnp.stack(pw_im).reshape(V7X_SUBLANES, n_tiles, S5_TILE_STATES)
    row = jnp.arange(V7X_SUBLANES)[:, None, None]
    tabs = []
    for s in S5_STEPS:
        keep = row >= s
        tabs.append(jnp.concatenate([jnp.where(keep, pw_re[s - 1][None], 0.0),
                                     jnp.where(keep, pw_im[s - 1][None], 0.0)], axis=-1))
    tabs.append(jnp.concatenate([pw_re, pw_im], axis=-1))
    tab = jnp.stack(tabs).transpose(2, 0, 1, 3)
    return bw, cw, tab


def _s5_scan(x, gain, bw, cw, tab, d_skip, *, rows):
    bsz, seqlen, d = x.shape
    n_tiles = bw.shape[0]
    two_ns = 2 * S5_TILE_STATES
    return pl.pallas_call(
        functools.partial(_s5_kernel, rows=rows),
        grid=(bsz, seqlen // rows),
        in_specs=[
            pl.BlockSpec((None, rows, d), lambda b, c: (b, c, 0)),
            _const_spec((1, d)),
            _const_spec(bw.shape),
            _const_spec(cw.shape),
            _const_spec(tab.shape),
            _const_spec((1, d)),
        ],
        out_specs=pl.BlockSpec((None, rows, d), lambda b, c: (b, c, 0)),
        out_shape=jax.ShapeDtypeStruct((bsz, seqlen, d), BF16),
        scratch_shapes=[
            pltpu.VMEM((n_tiles, V7X_SUBLANES, two_ns), F32),
            pltpu.VMEM((rows, two_ns), F32),
        ],
        compiler_params=_params(("arbitrary", "arbitrary")),
        name="s5_scan",
    )(x, gain.reshape(1, d), bw, cw, tab, d_skip.reshape(1, d))


def _sigmoid(x):
    return 1.0 / (1.0 + jnp.exp(-x))


def _glu_kernel(g_ref, wv_ref, wg_ref, x_ref, o_ref):
    g = g_ref[...]
    val = jnp.dot(g, wv_ref[...], preferred_element_type=F32)
    gate = jnp.dot(g, wg_ref[...], preferred_element_type=F32)
    o_ref[...] = x_ref[...] + val * _sigmoid(gate)


def _glu_proj(g, w_glu, x, *, tm, tn):
    t, d = x.shape
    nj = d // tn
    return pl.pallas_call(
        _glu_kernel,
        grid=(t // tm, nj),
        in_specs=[
            pl.BlockSpec((tm, d), lambda i, j: (i, 0)),
            pl.BlockSpec((d, tn), lambda i, j: (0, j)),
            pl.BlockSpec((d, tn), lambda i, j: (0, j + nj)),
            pl.BlockSpec((tm, tn), lambda i, j: (i, j)),
        ],
        out_specs=pl.BlockSpec((tm, tn), lambda i, j: (i, j)),
        out_shape=jax.ShapeDtypeStruct((t, d), F32),
        compiler_params=_params(("arbitrary", "arbitrary")),
        name="glu_proj",
    )(g, w_glu, w_glu, x)


def _ffn_kernel(x_ref, gain_ref, wg_ref, wu_ref, wd_ref, o_ref, h_ref, acc_ref):
    j = pl.program_id(1)

    @pl.when(j == 0)
    def _():
        x = x_ref[...]
        h_ref[...] = (x * _rms_scale(x) * gain_ref[...]).astype(BF16)

    h = h_ref[...]
    gate = jnp.dot(h, wg_ref[...], preferred_element_type=F32)
    up = jnp.dot(h, wu_ref[...], preferred_element_type=F32)
    act = (gate * _sigmoid(gate) * up).astype(BF16)
    part = jnp.dot(act, wd_ref[...], preferred_element_type=F32)

    @pl.when(j == 0)
    def _():
        acc_ref[...] = part

    @pl.when(j > 0)
    def _():
        acc_ref[...] += part

    @pl.when(j == pl.num_programs(1) - 1)
    def _():
        o_ref[...] = x_ref[...] + acc_ref[...]


def _ffn(x, gain, w_gate_up, w_down, *, tm, tf):
    t, d = x.shape
    d_ff = w_down.shape[0]
    nj = d_ff // tf
    return pl.pallas_call(
        _ffn_kernel,
        grid=(t // tm, nj),
        in_specs=[
            pl.BlockSpec((tm, d), lambda i, j: (i, 0)),
            _const_spec((1, d)),
            pl.BlockSpec((d, tf), lambda i, j: (0, j)),
            pl.BlockSpec((d, tf), lambda i, j: (0, j + nj)),
            pl.BlockSpec((tf, d), lambda i, j: (j, 0)),
        ],
        out_specs=pl.BlockSpec((tm, d), lambda i, j: (i, 0)),
        out_shape=jax.ShapeDtypeStruct((t, d), F32),
        scratch_shapes=[pltpu.VMEM((tm, d), BF16), pltpu.VMEM((tm, d), F32)],
        compiler_params=_params(("arbitrary", "arbitrary")),
        name="ffn",
    )(x, gain.reshape(1, d), w_gate_up, w_gate_up, w_down)


def _head_norm(y, gain):
    parts = []
    for a in range(y.shape[1] // HEAD_DIM):
        ya = y[:, a * HEAD_DIM:(a + 1) * HEAD_DIM]
        parts.append(ya * _rms_scale(ya))
    return jnp.concatenate(parts, axis=1) * gain


def _qkv_kernel(x_ref, gains_ref, w_ref, hgain_ref, o_ref, hq_ref, hk_ref, *, n_q_blocks, n_k_blocks):
    j = pl.program_id(1)

    @pl.when(j == 0)
    def _():
        x = x_ref[...]
        xn = x * _rms_scale(x)
        hq_ref[...] = (xn * gains_ref[0:1, :]).astype(BF16)
        hk_ref[...] = (xn * gains_ref[1:2, :]).astype(BF16)

    @pl.when(j < n_q_blocks)
    def _():
        y = jnp.dot(hq_ref[...], w_ref[...], preferred_element_type=F32)
        o_ref[...] = _head_norm(y, hgain_ref[0])

    @pl.when((j >= n_q_blocks) & (j < n_q_blocks + n_k_blocks))
    def _():
        y = jnp.dot(hk_ref[...], w_ref[...], preferred_element_type=F32)
        o_ref[...] = _head_norm(y, hgain_ref[0])

    @pl.when(j >= n_q_blocks + n_k_blocks)
    def _():
        o_ref[...] = jnp.dot(hk_ref[...], w_ref[...], preferred_element_type=F32)


def _qkv_proj(x, gains, w_all, hgain, *, tm, tn, n_q_blocks, n_k_blocks):
    t, d = x.shape
    n_out = w_all.shape[1]
    return pl.pallas_call(
        functools.partial(_qkv_kernel, n_q_blocks=n_q_blocks, n_k_blocks=n_k_blocks),
        grid=(t // tm, n_out // tn),
        in_specs=[
            pl.BlockSpec((tm, d), lambda i, j: (i, 0)),
            _const_spec((2, d)),
            pl.BlockSpec((d, tn), lambda i, j: (0, j)),
            pl.BlockSpec((1, 1, tn), lambda i, j: (j, 0, 0)),
        ],
        out_specs=pl.BlockSpec((tm, tn), lambda i, j: (i, j)),
        out_shape=jax.ShapeDtypeStruct((t, n_out), F32),
        scratch_shapes=[pltpu.VMEM((tm, d), BF16), pltpu.VMEM((tm, d), BF16)],
        compiler_params=_params(("arbitrary", "arbitrary")),
        name="qkv_proj",
    )(x, gains, w_all, hgain)


ATTN_SPAN = ATTN_BLOCK * max(d for _, d in DILATION_CFG)


def _attn_kernel(slope_ref, q0_ref, q1_ref, q2_ref, kc_ref, kp_ref, vc_ref, vp_ref, o_ref,
                 k_ref, v_ref, og_ref, lg_ref):
    s_idx = pl.program_id(1)
    head = pl.program_id(2)
    span = ATTN_SPAN
    blk = ATTN_BLOCK

    k_ref[pl.ds(0, span), :] = kp_ref[...]
    k_ref[pl.ds(span, span), :] = kc_ref[...]
    v_ref[pl.ds(0, span), :] = vp_ref[...]
    v_ref[pl.ds(span, span), :] = vc_ref[...]

    qi = lax.broadcasted_iota(jnp.int32, (blk, 2 * blk), 0) + blk
    kj = lax.broadcasted_iota(jnp.int32, (blk, 2 * blk), 1)
    dist = qi - kj
    in_prev = jnp.where(kj < blk, 1.0, 0.0).astype(F32)

    for g, (q_ref, (window, dil)) in enumerate(zip((q0_ref, q1_ref, q2_ref), DILATION_CFG)):
        w_sub = window // dil
        step = blk * dil
        n_blk = span // step
        slope = slope_ref[g, head]
        bias = jnp.where((dist >= 0) & (dist <= w_sub), dist.astype(F32) * (-slope * dil), NEG_BIG)

        def block(idx, carry, q_ref=q_ref, dil=dil, step=step, n_blk=n_blk, bias=bias, g=g):
            r = idx % dil
            nb = idx // dil
            row0 = nb * step + r
            q = q_ref[pl.ds(row0, blk, stride=dil), :].astype(BF16)
            kp = k_ref[pl.ds(span + row0 - step, blk, stride=dil), :]
            kc = k_ref[pl.ds(span + row0, blk, stride=dil), :]
            vp = v_ref[pl.ds(span + row0 - step, blk, stride=dil), :]
            vc = v_ref[pl.ds(span + row0, blk, stride=dil), :]
            kk = jnp.concatenate([kp, kc], axis=0).astype(BF16)
            vv = jnp.concatenate([vp, vc], axis=0).astype(BF16)
            s = lax.dot_general(q, kk, (((1,), (1,)), ((), ())), preferred_element_type=F32)
            no_prev = jnp.where((s_idx * n_blk + nb) == 0, NEG_BIG, 0.0).astype(F32)
            s = s + (bias + in_prev * no_prev)
            m = jnp.max(s, axis=-1, keepdims=True)
            p = jnp.exp(s - m)
            l = jnp.sum(p, axis=-1, keepdims=True)
            o = jnp.dot(p.astype(BF16), vv, preferred_element_type=F32) / l
            og_ref[g, pl.ds(row0, blk, stride=dil), :] = o
            lg_ref[g, pl.ds(row0, blk, stride=dil), :] = jnp.broadcast_to(m + jnp.log(l), (blk, HEAD_DIM))
            return carry

        lax.fori_loop(0, dil * n_blk, block, 0)

    lses = [lg_ref[g] for g in range(N_GROUPS)]
    top = functools.reduce(jnp.maximum, lses)
    wts = [jnp.exp(l - top) for l in lses]
    num = sum(w * og_ref[g] for g, w in enumerate(wts))
    o_ref[...] = (num / sum(wts)).astype(o_ref.dtype)


def _dilated_attn(qkv, slopes, *, n_heads):
    bsz, seqlen, _ = qkv.shape
    span = ATTN_SPAN
    k_col = N_GROUPS * n_heads
    v_col = k_col + n_heads
    blk = (None, span, HEAD_DIM)

    def col(base, prev=False):
        if prev:
            return lambda b, s, h: (b, jnp.maximum(s - 1, 0), base + h)
        return lambda b, s, h: (b, s, base + h)

    return pl.pallas_call(
        _attn_kernel,
        grid=(bsz, seqlen // span, n_heads),
        in_specs=[
            pl.BlockSpec(memory_space=pltpu.SMEM),
            pl.BlockSpec(blk, col(0)),
            pl.BlockSpec(blk, col(n_heads)),
            pl.BlockSpec(blk, col(2 * n_heads)),
            pl.BlockSpec(blk, col(k_col)),
            pl.BlockSpec(blk, col(k_col, prev=True)),
            pl.BlockSpec(blk, col(v_col)),
            pl.BlockSpec(blk, col(v_col, prev=True)),
        ],
        out_specs=pl.BlockSpec(blk, lambda b, s, h: (b, s, h)),
        out_shape=jax.ShapeDtypeStruct((bsz, seqlen, n_heads * HEAD_DIM), BF16),
        scratch_shapes=[
            pltpu.VMEM((2 * span, HEAD_DIM), F32),
            pltpu.VMEM((2 * span, HEAD_DIM), F32),
            pltpu.VMEM((N_GROUPS, span, HEAD_DIM), F32),
            pltpu.VMEM((N_GROUPS, span, HEAD_DIM), F32),
        ],
        compiler_params=_params(("arbitrary", "arbitrary", "arbitrary")),
        name="dilated_attn",
    )(slopes, qkv, qkv, qkv, qkv, qkv, qkv, qkv)


def _out_kernel(a_ref, w_ref, x_ref, o_ref):
    o_ref[...] = x_ref[...] + jnp.dot(a_ref[...], w_ref[...], preferred_element_type=F32)


def _out_proj(a, w, x, *, tm, tn):
    t, d = x.shape
    k = a.shape[1]
    return pl.pallas_call(
        _out_kernel,
        grid=(t // tm, d // tn),
        in_specs=[
            pl.BlockSpec((tm, k), lambda i, j: (i, 0)),
            pl.BlockSpec((k, tn), lambda i, j: (0, j)),
            pl.BlockSpec((tm, tn), lambda i, j: (i, j)),
        ],
        out_specs=pl.BlockSpec((tm, tn), lambda i, j: (i, j)),
        out_shape=jax.ShapeDtypeStruct((t, d), F32),
        compiler_params=_params(("arbitrary", "arbitrary")),
        name="out_proj",
    )(a, w, x)


def _alibi_slopes(n_heads):
    n = N_GROUPS * n_heads
    i = jnp.arange(1, n + 1, dtype=F32)
    return jnp.exp2(-8.0 * i / n).reshape(n_heads, N_GROUPS).T


def kernel(x, mix_norm, ffn_norm, ssm_a_re, ssm_a_im, ssm_log_dt, ssm_b_re, ssm_b_im, ssm_c_re, ssm_c_im,
           ssm_d, w_glu, kv_norm, w_kv, k_norm, w_q, q_norm, w_o, w_gate_up, w_down):
    bsz, seqlen, d = x.shape
    t = bsz * seqlen
    n_heads = w_o.shape[1] // HEAD_DIM
    assert ssm_a_re.shape[0] == 1 and w_q.shape[0] == 1, "one S5 layer followed by one attention layer"
    assert seqlen % ATTN_SPAN == 0

    bw, cw, tab = _s5_prepare(ssm_a_re[0], ssm_a_im[0], ssm_log_dt[0], ssm_b_re[0], ssm_b_im[0],
                              ssm_c_re[0], ssm_c_im[0])
    g = _s5_scan(x, mix_norm[0], bw, cw, tab, ssm_d[0], rows=256)
    x = x.reshape(t, d)
    x = _glu_proj(g.reshape(t, d), w_glu[0].astype(BF16), x, tm=1024, tn=512)
    x = _ffn(x, ffn_norm[0], w_gate_up[0].astype(BF16), w_down[0].astype(BF16), tm=512, tf=512)

    tn = 4 * HEAD_DIM
    heads_per_block = tn // HEAD_DIM
    w_all = jnp.concatenate([w_q[0], w_kv], axis=1).astype(BF16)
    q_gain = jnp.repeat(q_norm[0].astype(F32) * (HEAD_DIM ** -0.5), n_heads, axis=0)
    k_gain = jnp.tile(k_norm.astype(F32)[None], (n_heads, 1))
    v_gain = jnp.ones((n_heads, HEAD_DIM), F32)
    hgain = jnp.concatenate([q_gain, k_gain, v_gain], axis=0).reshape(-1, 1, heads_per_block * HEAD_DIM)
    gains = jnp.stack([mix_norm[1], kv_norm]).astype(F32)
    qkv = _qkv_proj(x, gains, w_all, hgain, tm=1024, tn=tn,
                    n_q_blocks=N_GROUPS * n_heads // heads_per_block, n_k_blocks=n_heads // heads_per_block)
    merged = _dilated_attn(qkv.reshape(bsz, seqlen, -1), _alibi_slopes(n_heads), n_heads=n_heads)
    x = _out_proj(merged.reshape(t, n_heads * HEAD_DIM), w_o[0].astype(BF16), x, tm=1024, tn=1024)
    x = _ffn(x, ffn_norm[1], w_gate_up[1].astype(BF16), w_down[1].astype(BF16), tm=512, tf=512)
    return x.reshape(bsz, seqlen, d)
```

```python
import functools
import math

import jax
import jax.numpy as jnp
from jax import lax
from jax.experimental import pallas as pl
from jax.experimental.pallas import tpu as pltpu

F32 = jnp.float32
BF16 = jnp.bfloat16

EPS = 1e-6
SSM_GROUP = 16
SSM_STATE = 64
HEAD_DIM = 128
ATTN_BLOCK = 128
DILATION_CFG = ((128, 1), (512, 4), (2048, 16))
N_GROUPS = len(DILATION_CFG)
NEG_BIG = -1e30

V7X_SUBLANES = 8
V7X_LANES = 128
V7X_MXU_DIM = 256
V7X_VMEM_BYTES = 64 * 1024 * 1024
VMEM_LIMIT = V7X_VMEM_BYTES - 8 * 1024 * 1024


def _params(semantics):
    return pltpu.CompilerParams(dimension_semantics=semantics, vmem_limit_bytes=VMEM_LIMIT)


def _const_spec(shape):
    nd = len(shape)
    return pl.BlockSpec(shape, lambda *_: (0,) * nd, pipeline_mode=pl.Buffered(1))


def _rms_scale(x):
    return lax.rsqrt(jnp.mean(x * x, axis=-1, keepdims=True) + EPS)


S5_TILE_GROUPS = V7X_MXU_DIM // SSM_GROUP
S5_TILE_STATES = S5_TILE_GROUPS * SSM_STATE
S5_STEPS = (1, 2, 4)


def _s5_kernel(x_ref, gain_ref, bw_ref, cw_ref, tab_ref, dskip_ref, o_ref, h_ref, slab_ref, *, rows):
    ns = S5_TILE_STATES
    n_tiles = bw_ref.shape[0]

    @pl.when(pl.program_id(1) == 0)
    def _():
        h_ref[...] = jnp.zeros_like(h_ref)

    x = x_ref[...]
    u = x * _rms_scale(x) * gain_ref[...]
    u_bf = u.astype(BF16)

    for m in range(n_tiles):
        lo = m * V7X_MXU_DIM
        hi = lo + V7X_MXU_DIM
        slab_ref[...] = jnp.dot(u_bf[:, lo:hi], bw_ref[m], preferred_element_type=F32)

        def tile_scan(j, h, m=m):
            r0 = pl.multiple_of(j * V7X_SUBLANES, V7X_SUBLANES)
            xr = slab_ref[pl.ds(r0, V7X_SUBLANES), :ns]
            xi = slab_ref[pl.ds(r0, V7X_SUBLANES), ns:]
            for t, s in enumerate(S5_STEPS):
                ar = tab_ref[m, t, :, :ns]
                ai = tab_ref[m, t, :, ns:]
                rr = pltpu.roll(xr, s, 0)
                ri = pltpu.roll(xi, s, 0)
                xr, xi = xr + (ar * rr - ai * ri), xi + (ar * ri + ai * rr)
            pr = tab_ref[m, len(S5_STEPS), :, :ns]
            pi = tab_ref[m, len(S5_STEPS), :, ns:]
            hr = h[:, :ns]
            hi_ = h[:, ns:]
            xr = xr + (pr * hr - pi * hi_)
            xi = xi + (pr * hi_ + pi * hr)
            slab_ref[pl.ds(r0, V7X_SUBLANES), :ns] = xr
            slab_ref[pl.ds(r0, V7X_SUBLANES), ns:] = xi
            last = jnp.concatenate([xr[V7X_SUBLANES - 1:, :], xi[V7X_SUBLANES - 1:, :]], axis=1)
            return jnp.broadcast_to(last, (V7X_SUBLANES, 2 * ns))

        h_ref[m] = lax.fori_loop(0, rows // V7X_SUBLANES, tile_scan, h_ref[m])

        y = jnp.dot(slab_ref[...].astype(BF16), cw_ref[m], preferred_element_type=F32)
        y = y + dskip_ref[:, lo:hi] * u[:, lo:hi]
        o_ref[:, lo:hi] = jax.nn.gelu(y).astype(o_ref.dtype)


def _s5_prepare(a_re, a_im, log_dt, b_re, b_im, c_re, c_im):
    lam_re = jnp.minimum(a_re.astype(F32), -1e-4)
    lam_im = a_im.astype(F32)
    dt = jnp.exp(log_dt.astype(F32))[:, None]
    mag = jnp.exp(lam_re * dt)
    ang = lam_im * dt
    lb_re = mag * jnp.cos(ang)
    lb_im = mag * jnp.sin(ang)
    den = lam_re * lam_re + lam_im * lam_im
    nr = lb_re - 1.0
    ni = lb_im
    coef_re = (nr * lam_re + ni * lam_im) / den
    coef_im = (ni * lam_re - nr * lam_im) / den
    br = b_re.astype(F32)
    bi = b_im.astype(F32)
    bb_re = coef_re[..., None] * br - coef_im[..., None] * bi
    bb_im = coef_re[..., None] * bi + coef_im[..., None] * br

    n_groups = a_re.shape[0]
    n_tiles = n_groups // S5_TILE_GROUPS
    eye = jnp.eye(S5_TILE_GROUPS, dtype=F32)

    def pack_b(w):
        w = w.reshape(n_tiles, S5_TILE_GROUPS, SSM_STATE, SSM_GROUP)
        return jnp.einsum('mgph,gk->mghkp', w, eye).reshape(n_tiles, V7X_MXU_DIM, S5_TILE_STATES)

    def pack_c(w):
        w = w.reshape(n_tiles, S5_TILE_GROUPS, SSM_GROUP, SSM_STATE)
        return jnp.einsum('mghp,gk->mgpkh', w, eye).reshape(n_tiles, S5_TILE_STATES, V7X_MXU_DIM)

    bw = jnp.concatenate([pack_b(bb_re), pack_b(bb_im)], axis=-1).astype(BF16)
    cw = jnp.concatenate([pack_c(c_re.astype(F32)), -pack_c(c_im.astype(F32))], axis=1).astype(BF16)

    pw_re = [lb_re]
    pw_im = [lb_im]
    for _ in range(V7X_SUBLANES - 1):
        pr, pi = pw_re[-1], pw_im[-1]
        pw_re.append(pr * lb_re - pi * lb_im)
        pw_im.append(pr * lb_im + pi * lb_re)
    pw_re = jnp.stack(pw_re).reshape(V7X_SUBLANES, n_tiles, S5_TILE_STATES)
    pw_im = jnp.stack(pw_im).reshape(V7X_SUBLANES, n_tiles, S5_TILE_STATES)
    row = jnp.arange(V7X_SUBLANES)[:, None, None]
    tabs = []
    for s in S5_STEPS:
        keep = row >= s
        tabs.append(jnp.concatenate([jnp.where(keep, pw_re[s - 1][None], 0.0),
                                     jnp.where(keep, pw_im[s - 1][None], 0.0)], axis=-1))
    tabs.append(jnp.concatenate([pw_re, pw_im], axis=-1))
    tab = jnp.stack(tabs).transpose(2, 0, 1, 3)
    return bw, cw, tab


def _s5_scan(x, gain, bw, cw, tab, d_skip, *, rows):
    bsz, seqlen, d = x.shape
    n_tiles = bw.shape[0]
    two_ns = 2 * S5_TILE_STATES
    return pl.pallas_call(
        functools.partial(_s5_kernel, rows=rows),
        grid=(bsz, seqlen // rows),
        in_specs=[
            pl.BlockSpec((None, rows, d), lambda b, c: (b, c, 0)),
            _const_spec((1, d)),
            _const_spec(bw.shape),
            _const_spec(cw.shape),
            _const_spec(tab.shape),
            _const_spec((1, d)),
        ],
        out_specs=pl.BlockSpec((None, rows, d), lambda b, c: (b, c, 0)),
        out_shape=jax.ShapeDtypeStruct((bsz, seqlen, d), BF16),
        scratch_shapes=[
            pltpu.VMEM((n_tiles, V7X_SUBLANES, two_ns), F32),
            pltpu.VMEM((rows, two_ns), F32),
        ],
        compiler_params=_params(("arbitrary", "arbitrary")),
        name="s5_scan",
    )(x, gain.reshape(1, d), bw, cw, tab, d_skip.reshape(1, d))


def _sigmoid(x):
    return 1.0 / (1.0 + jnp.exp(-x))


def _glu_kernel(g_ref, wv_ref, wg_ref, x_ref, o_ref):
    g = g_ref[...]
    val = jnp.dot(g, wv_ref[...], preferred_element_type=F32)
    gate = jnp.dot(g, wg_ref[...], preferred_element_type=F32)
    o_ref[...] = x_ref[...] + val * _sigmoid(gate)


def _glu_proj(g, w_glu, x, *, tm, tn):
    t, d = x.shape
    nj = d // tn
    return pl.pallas_call(
        _glu_kernel,
        grid=(t // tm, nj),
        in_specs=[
            pl.BlockSpec((tm, d), lambda i, j: (i, 0)),
            pl.BlockSpec((d, tn), lambda i, j: (0, j)),
            pl.BlockSpec((d, tn), lambda i, j: (0, j + nj)),
            pl.BlockSpec((tm, tn), lambda i, j: (i, j)),
        ],
        out_specs=pl.BlockSpec((tm, tn), lambda i, j: (i, j)),
        out_shape=jax.ShapeDtypeStruct((t, d), F32),
        compiler_params=_params(("arbitrary", "arbitrary")),
        name="glu_proj",
    )(g, w_glu, w_glu, x)


def _ffn_kernel(x_ref, gain_ref, wg_ref, wu_ref, wd_ref, o_ref, h_ref, acc_ref):
    j = pl.program_id(1)

    @pl.when(j == 0)
    def _():
        x = x_ref[...]
        h_ref[...] = (x * _rms_scale(x) * gain_ref[...]).astype(BF16)
        acc_ref[...] = jnp.zeros_like(acc_ref)

    h = h_ref[...]
    gate = jnp.dot(h, wg_ref[...], preferred_element_type=F32)
    up = jnp.dot(h, wu_ref[...], preferred_element_type=F32)
    act = (gate * _sigmoid(gate) * up).astype(BF16)
    acc_ref[...] += jnp.dot(act, wd_ref[...], preferred_element_type=F32)

    @pl.when(j == pl.num_programs(1) - 1)
    def _():
        o_ref[...] = x_ref[...] + acc_ref[...]


def _ffn(x, gain, w_gate_up, w_down, *, tm, tf):
    t, d = x.shape
    d_ff = w_down.shape[0]
    nj = d_ff // tf
    return pl.pallas_call(
        _ffn_kernel,
        grid=(t // tm, nj),
        in_specs=[
            pl.BlockSpec((tm, d), lambda i, j: (i, 0)),
            _const_spec((1, d)),
            pl.BlockSpec((d, tf), lambda i, j: (0, j)),
            pl.BlockSpec((d, tf), lambda i, j: (0, j + nj)),
            pl.BlockSpec((tf, d), lambda i, j: (j, 0)),
        ],
        out_specs=pl.BlockSpec((tm, d), lambda i, j: (i, 0)),
        out_shape=jax.ShapeDtypeStruct((t, d), F32),
        scratch_shapes=[pltpu.VMEM((tm, d), BF16), pltpu.VMEM((tm, d), F32)],
        compiler_params=_params(("arbitrary", "arbitrary")),
        name="ffn",
    )(x, gain.reshape(1, d), w_gate_up, w_gate_up, w_down)


def _head_norm(y, gain):
    parts = []
    for a in range(y.shape[1] // HEAD_DIM):
        ya = y[:, a * HEAD_DIM:(a + 1) * HEAD_DIM]
        parts.append(ya * _rms_scale(ya))
    return jnp.concatenate(parts, axis=1) * gain


def _qkv_kernel(x_ref, gains_ref, w_ref, hgain_ref, o_ref, hq_ref, hk_ref, *, n_q_blocks, n_k_blocks):
    j = pl.program_id(1)

    @pl.when(j == 0)
    def _():
        x = x_ref[...]
        xn = x * _rms_scale(x)
        hq_ref[...] = (xn * gains_ref[0:1, :]).astype(BF16)
        hk_ref[...] = (xn * gains_ref[1:2, :]).astype(BF16)

    @pl.when(j < n_q_blocks)
    def _():
        y = jnp.dot(hq_ref[...], w_ref[...], preferred_element_type=F32)
        o_ref[...] = _head_norm(y, hgain_ref[0])

    @pl.when((j >= n_q_blocks) & (j < n_q_blocks + n_k_blocks))
    def _():
        y = jnp.dot(hk_ref[...], w_ref[...], preferred_element_type=F32)
        o_ref[...] = _head_norm(y, hgain_ref[0])

    @pl.when(j >= n_q_blocks + n_k_blocks)
    def _():
        o_ref[...] = jnp.dot(hk_ref[...], w_ref[...], preferred_element_type=F32)


def _qkv_proj(x, gains, w_all, hgain, *, tm, tn, n_q_blocks, n_k_blocks):
    t, d = x.shape
    n_out = w_all.shape[1]
    return pl.pallas_call(
        functools.partial(_qkv_kernel, n_q_blocks=n_q_blocks, n_k_blocks=n_k_blocks),
        grid=(t // tm, n_out // tn),
        in_specs=[
            pl.BlockSpec((tm, d), lambda i, j: (i, 0)),
            _const_spec((2, d)),
            pl.BlockSpec((d, tn), lambda i, j: (0, j)),
            pl.BlockSpec((1, 1, tn), lambda i, j: (j, 0, 0)),
        ],
        out_specs=pl.BlockSpec((tm, tn), lambda i, j: (i, j)),
        out_shape=jax.ShapeDtypeStruct((t, n_out), F32),
        scratch_shapes=[pltpu.VMEM((tm, d), BF16), pltpu.VMEM((tm, d), BF16)],
        compiler_params=_params(("arbitrary", "arbitrary")),
        name="qkv_proj",
    )(x, gains, w_all, hgain)


ATTN_SPAN = ATTN_BLOCK * max(d for _, d in DILATION_CFG)


def _attn_kernel(slope_ref, q0_ref, q1_ref, q2_ref, kc_ref, kp_ref, vc_ref, vp_ref, o_ref,
                 k_ref, v_ref, og_ref, lg_ref):
    s_idx = pl.program_id(1)
    head = pl.program_id(2)
    span = ATTN_SPAN
    blk = ATTN_BLOCK

    k_ref[pl.ds(0, span), :] = kp_ref[...]
    k_ref[pl.ds(span, span), :] = kc_ref[...]
    v_ref[pl.ds(0, span), :] = vp_ref[...]
    v_ref[pl.ds(span, span), :] = vc_ref[...]

    qi = lax.broadcasted_iota(jnp.int32, (blk, 2 * blk), 0) + blk
    kj = lax.broadcasted_iota(jnp.int32, (blk, 2 * blk), 1)
    dist = qi - kj
    in_prev = jnp.where(kj < blk, 1.0, 0.0).astype(F32)

    for g, (q_ref, (window, dil)) in enumerate(zip((q0_ref, q1_ref, q2_ref), DILATION_CFG)):
        w_sub = window // dil
        step = blk * dil
        n_blk = span // step
        slope = slope_ref[g, head]
        bias = jnp.where((dist >= 0) & (dist <= w_sub), dist.astype(F32) * (-slope * dil), NEG_BIG)

        def block(idx, carry, q_ref=q_ref, dil=dil, step=step, n_blk=n_blk, bias=bias, g=g):
            r = idx % dil
            nb = idx // dil
            row0 = nb * step + r
            q = q_ref[pl.ds(row0, blk, stride=dil), :].astype(BF16)
            kp = k_ref[pl.ds(span + row0 - step, blk, stride=dil), :]
            kc = k_ref[pl.ds(span + row0, blk, stride=dil), :]
            vp = v_ref[pl.ds(span + row0 - step, blk, stride=dil), :]
            vc = v_ref[pl.ds(span + row0, blk, stride=dil), :]
            kk = jnp.concatenate([kp, kc], axis=0).astype(BF16)
            vv = jnp.concatenate([vp, vc], axis=0).astype(BF16)
            s = lax.dot_general(q, kk, (((1,), (1,)), ((), ())), preferred_element_type=F32)
            no_prev = jnp.where((s_idx * n_blk + nb) == 0, NEG_BIG, 0.0).astype(F32)
            s = s + (bias + in_prev * no_prev)
            m = jnp.max(s, axis=-1, keepdims=True)
            p = jnp.exp(s - m)
            l = jnp.sum(p, axis=-1, keepdims=True)
            o = jnp.dot(p.astype(BF16), vv, preferred_element_type=F32) / l
            og_ref[g, pl.ds(row0, blk, stride=dil), :] = o
            lg_ref[g, pl.ds(row0, blk, stride=dil), :] = jnp.broadcast_to(m + jnp.log(l), (blk, HEAD_DIM))
            return carry

        lax.fori_loop(0, dil * n_blk, block, 0, unroll=8)

    lses = [lg_ref[g] for g in range(N_GROUPS)]
    top = functools.reduce(jnp.maximum, lses)
    wts = [jnp.exp(l - top) for l in lses]
    num = sum(w * og_ref[g] for g, w in enumerate(wts))
    o_ref[...] = (num / sum(wts)).astype(o_ref.dtype)


def _dilated_attn(qkv, slopes, *, n_heads):
    bsz, seqlen, _ = qkv.shape
    span = ATTN_SPAN
    k_col = N_GROUPS * n_heads
    v_col = k_col + n_heads
    blk = (None, span, HEAD_DIM)

    def col(base, prev=False):
        if prev:
            return lambda b, s, h: (b, jnp.maximum(s - 1, 0), base + h)
        return lambda b, s, h: (b, s, base + h)

    return pl.pallas_call(
        _attn_kernel,
        grid=(bsz, seqlen // span, n_heads),
        in_specs=[
            pl.BlockSpec(memory_space=pltpu.SMEM),
            pl.BlockSpec(blk, col(0)),
            pl.BlockSpec(blk, col(n_heads)),
            pl.BlockSpec(blk, col(2 * n_heads)),
            pl.BlockSpec(blk, col(k_col)),
            pl.BlockSpec(blk, col(k_col, prev=True)),
            pl.BlockSpec(blk, col(v_col)),
            pl.BlockSpec(blk, col(v_col, prev=True)),
        ],
        out_specs=pl.BlockSpec(blk, lambda b, s, h: (b, s, h)),
        out_shape=jax.ShapeDtypeStruct((bsz, seqlen, n_heads * HEAD_DIM), BF16),
        scratch_shapes=[
            pltpu.VMEM((2 * span, HEAD_DIM), F32),
            pltpu.VMEM((2 * span, HEAD_DIM), F32),
            pltpu.VMEM((N_GROUPS, span, HEAD_DIM), F32),
            pltpu.VMEM((N_GROUPS, span, HEAD_DIM), F32),
        ],
        compiler_params=_params(("arbitrary", "arbitrary", "arbitrary")),
        name="dilated_attn",
    )(slopes, qkv, qkv, qkv, qkv, qkv, qkv, qkv)


def _out_kernel(a_ref, w_ref, x_ref, o_ref):
    o_ref[...] = x_ref[...] + jnp.dot(a_ref[...], w_ref[...], preferred_element_type=F32)


def _out_proj(a, w, x, *, tm, tn):
    t, d = x.shape
    k = a.shape[1]
    return pl.pallas_call(
        _out_kernel,
        grid=(t // tm, d // tn),
        in_specs=[
            pl.BlockSpec((tm, k), lambda i, j: (i, 0)),
            pl.BlockSpec((k, tn), lambda i, j: (0, j)),
            pl.BlockSpec((tm, tn), lambda i, j: (i, j)),
        ],
        out_specs=pl.BlockSpec((tm, tn), lambda i, j: (i, j)),
        out_shape=jax.ShapeDtypeStruct((t, d), F32),
        compiler_params=_params(("arbitrary", "arbitrary")),
        name="out_proj",
    )(a, w, x)


def _alibi_slopes(n_heads):
    n = N_GROUPS * n_heads
    i = jnp.arange(1, n + 1, dtype=F32)
    return jnp.exp2(-8.0 * i / n).reshape(n_heads, N_GROUPS).T


def kernel(x, mix_norm, ffn_norm, ssm_a_re, ssm_a_im, ssm_log_dt, ssm_b_re, ssm_b_im, ssm_c_re, ssm_c_im,
           ssm_d, w_glu, kv_norm, w_kv, k_norm, w_q, q_norm, w_o, w_gate_up, w_down):
    bsz, seqlen, d = x.shape
    t = bsz * seqlen
    n_heads = w_o.shape[1] // HEAD_DIM
    assert ssm_a_re.shape[0] == 1 and w_q.shape[0] == 1, "one S5 layer followed by one attention layer"
    assert seqlen % ATTN_SPAN == 0

    bw, cw, tab = _s5_prepare(ssm_a_re[0], ssm_a_im[0], ssm_log_dt[0], ssm_b_re[0], ssm_b_im[0],
                              ssm_c_re[0], ssm_c_im[0])
    g = _s5_scan(x, mix_norm[0], bw, cw, tab, ssm_d[0], rows=256)
    x = x.reshape(t, d)
    x = _glu_proj(g.reshape(t, d), w_glu[0].astype(BF16), x, tm=1024, tn=512)
    x = _ffn(x, ffn_norm[0], w_gate_up[0].astype(BF16), w_down[0].astype(BF16), tm=512, tf=512)

    tn = 4 * HEAD_DIM
    heads_per_block = tn // HEAD_DIM
    w_all = jnp.concatenate([w_q[0], w_kv], axis=1).astype(BF16)
    q_gain = jnp.repeat(q_norm[0].astype(F32) * (HEAD_DIM ** -0.5), n_heads, axis=0)
    k_gain = jnp.tile(k_norm.astype(F32)[None], (n_heads, 1))
    v_gain = jnp.ones((n_heads, HEAD_DIM), F32)
    hgain = jnp.concatenate([q_gain, k_gain, v_gain], axis=0).reshape(-1, 1, heads_per_block * HEAD_DIM)
    gains = jnp.stack([mix_norm[1], kv_norm]).astype(F32)
    qkv = _qkv_proj(x, gains, w_all, hgain, tm=1024, tn=tn,
                    n_q_blocks=N_GROUPS * n_heads // heads_per_block, n_k_blocks=n_heads // heads_per_block)
    merged = _dilated_attn(qkv.reshape(bsz, seqlen, -1), _alibi_slopes(n_heads), n_heads=n_heads)
    x = _out_proj(merged.reshape(t, n_heads * HEAD_DIM), w_o[0].astype(BF16), x, tm=1024, tn=1024)
    x = _ffn(x, ffn_norm[1], w_gate_up[1].astype(BF16), w_down[1].astype(BF16), tm=512, tf=512)
    return x.reshape(bsz, seqlen, d)
```

```python
import functools
import math

import jax
import jax.numpy as jnp
from jax import lax
from jax.experimental import pallas as pl
from jax.experimental.pallas import tpu as pltpu

F32 = jnp.float32
BF16 = jnp.bfloat16

EPS = 1e-6
SSM_GROUP = 16
SSM_STATE = 64
HEAD_DIM = 128
ATTN_BLOCK = 128
DILATION_CFG = ((128, 1), (512, 4), (2048, 16))
N_GROUPS = len(DILATION_CFG)
NEG_BIG = -1e30

V7X_SUBLANES = 8
V7X_LANES = 128
V7X_MXU_DIM = 256
V7X_VMEM_BYTES = 64 * 1024 * 1024
VMEM_LIMIT = V7X_VMEM_BYTES - 8 * 1024 * 1024


def _params(semantics):
    return pltpu.CompilerParams(dimension_semantics=semantics, vmem_limit_bytes=VMEM_LIMIT)


def _const_spec(shape):
    nd = len(shape)
    return pl.BlockSpec(shape, lambda *_: (0,) * nd, pipeline_mode=pl.Buffered(1))


def _rms_scale(x):
    return lax.rsqrt(jnp.mean(x * x, axis=-1, keepdims=True) + EPS)


S5_TILE_GROUPS = V7X_MXU_DIM // SSM_GROUP
S5_TILE_STATES = S5_TILE_GROUPS * SSM_STATE
S5_STEPS = (1, 2, 4)
S5_SUPER = V7X_SUBLANES * V7X_SUBLANES
S5_TAB_POW = 0
S5_TAB_HS = V7X_SUBLANES
S5_TAB_CARRY = S5_TAB_HS + len(S5_STEPS)
S5_N_TABS = S5_TAB_CARRY + 1


def _cmul(ar, ai, br, bi):
    return ar * br - ai * bi, ar * bi + ai * br


def _s5_scan_column(slab_ref, tab_ref, m, c, row0, h):
    n_col = slab_ref.shape[0] // 2
    lanes = slice(c * V7X_LANES, (c + 1) * V7X_LANES)

    def tab(t):
        return tab_ref[m, t, 0, :, lanes], tab_ref[m, t, 1, :, lanes]

    def rows(k):
        return pl.ds(row0 + k * V7X_SUBLANES, V7X_SUBLANES)

    lam = tab(S5_TAB_POW)
    loc = []
    for k in range(V7X_SUBLANES):
        xr = slab_ref[c, rows(k), :]
        xi = slab_ref[n_col + c, rows(k), :]
        if k:
            pr, pi = _cmul(lam[0], lam[1], loc[-1][0], loc[-1][1])
            xr, xi = xr + pr, xi + pi
        loc.append((xr, xi))

    er, ei = loc[-1]
    for t, s in enumerate(S5_STEPS):
        ar, ai = tab(S5_TAB_HS + t)
        pr, pi = _cmul(ar, ai, pltpu.roll(er, s, 0), pltpu.roll(ei, s, 0))
        er, ei = er + pr, ei + pi
    cr, ci = tab(S5_TAB_CARRY)
    pr, pi = _cmul(cr, ci, h[0], h[1])
    er, ei = er + pr, ei + pi
    first = lax.broadcasted_iota(jnp.int32, er.shape, 0) == 0
    inr = jnp.where(first, h[0], pltpu.roll(er, 1, 0))
    ini = jnp.where(first, h[1], pltpu.roll(ei, 1, 0))

    for k in range(V7X_SUBLANES):
        qr, qi = tab(S5_TAB_POW + k)
        pr, pi = _cmul(qr, qi, inr, ini)
        slab_ref[c, rows(k), :] = loc[k][0] + pr
        slab_ref[n_col + c, rows(k), :] = loc[k][1] + pi

    last = V7X_SUBLANES - 1
    return (jnp.broadcast_to(er[last:, :], er.shape), jnp.broadcast_to(ei[last:, :], ei.shape))


def _tile_transpose(ref, rows):
    return jnp.concatenate([ref[pl.ds(row0 + k, V7X_SUBLANES, stride=V7X_SUBLANES), :]
                            for row0 in range(0, rows, S5_SUPER) for k in range(V7X_SUBLANES)], axis=0)


def _s5_kernel(x_ref, gain_ref, bw_ref, cw_ref, tab_ref, dskip_ref, o_ref,
               h_ref, slabs_ref, perm_ref, ubf_ref, yperm_ref, *, rows):
    n_tiles = bw_ref.shape[0]
    n_slab = slabs_ref.shape[1]
    n_col = n_slab // 2

    @pl.when(pl.program_id(1) == 0)
    def _():
        h_ref[...] = jnp.zeros_like(h_ref)

    x = x_ref[...]
    u = x * _rms_scale(x) * gain_ref[...]
    for s in range(perm_ref.shape[0]):
        perm_ref[s] = u[:, s * V7X_LANES:(s + 1) * V7X_LANES]
    for s in range(perm_ref.shape[0]):
        ubf_ref[:, s * V7X_LANES:(s + 1) * V7X_LANES] = _tile_transpose(perm_ref.at[s], rows).astype(BF16)

    for m in range(n_tiles):
        lo = m * V7X_MXU_DIM
        hi = lo + V7X_MXU_DIM
        slab_ref = slabs_ref.at[m % 2]
        bu = jnp.dot(ubf_ref[:, lo:hi], bw_ref[m], preferred_element_type=F32)
        for s in range(n_slab):
            slab_ref[s] = bu[:, s * V7X_LANES:(s + 1) * V7X_LANES]

        for c in range(n_col):
            h = (h_ref[m, 0, :, c * V7X_LANES:(c + 1) * V7X_LANES],
                 h_ref[m, 1, :, c * V7X_LANES:(c + 1) * V7X_LANES])
            for row0 in range(0, rows, S5_SUPER):
                h = _s5_scan_column(slab_ref, tab_ref, m, c, row0, h)
            h_ref[m, 0, :, c * V7X_LANES:(c + 1) * V7X_LANES] = h[0]
            h_ref[m, 1, :, c * V7X_LANES:(c + 1) * V7X_LANES] = h[1]

        st = jnp.concatenate([slab_ref[s] for s in range(n_slab)], axis=1).astype(BF16)
        y = jnp.dot(st, cw_ref[m], preferred_element_type=F32)
        y_ref = yperm_ref.at[m % 2]
        for s in range(y_ref.shape[0]):
            y_ref[s] = y[:, s * V7X_LANES:(s + 1) * V7X_LANES]
        y = jnp.concatenate([_tile_transpose(y_ref.at[s], rows) for s in range(y_ref.shape[0])], axis=1)
        y = y + dskip_ref[:, lo:hi] * u[:, lo:hi]
        o_ref[:, lo:hi] = jax.nn.gelu(y).astype(o_ref.dtype)


def _s5_prepare(a_re, a_im, log_dt, b_re, b_im, c_re, c_im):
    lam_re = jnp.minimum(a_re.astype(F32), -1e-4)
    lam_im = a_im.astype(F32)
    dt = jnp.exp(log_dt.astype(F32))[:, None]
    mag = jnp.exp(lam_re * dt)
    ang = lam_im * dt
    lb_re = mag * jnp.cos(ang)
    lb_im = mag * jnp.sin(ang)
    den = lam_re * lam_re + lam_im * lam_im
    nr = lb_re - 1.0
    ni = lb_im
    coef_re = (nr * lam_re + ni * lam_im) / den
    coef_im = (ni * lam_re - nr * lam_im) / den
    br = b_re.astype(F32)
    bi = b_im.astype(F32)
    bb_re = coef_re[..., None] * br - coef_im[..., None] * bi
    bb_im = coef_re[..., None] * bi + coef_im[..., None] * br

    n_groups = a_re.shape[0]
    n_tiles = n_groups // S5_TILE_GROUPS
    eye = jnp.eye(S5_TILE_GROUPS, dtype=F32)

    def pack_b(w):
        w = w.reshape(n_tiles, S5_TILE_GROUPS, SSM_STATE, SSM_GROUP)
        return jnp.einsum('mgph,gk->mghkp', w, eye).reshape(n_tiles, V7X_MXU_DIM, S5_TILE_STATES)

    def pack_c(w):
        w = w.reshape(n_tiles, S5_TILE_GROUPS, SSM_GROUP, SSM_STATE)
        return jnp.einsum('mghp,gk->mgpkh', w, eye).reshape(n_tiles, S5_TILE_STATES, V7X_MXU_DIM)

    bw = jnp.concatenate([pack_b(bb_re), pack_b(bb_im)], axis=-1).astype(BF16)
    cw = jnp.concatenate([pack_c(c_re.astype(F32)), -pack_c(c_im.astype(F32))], axis=1).astype(BF16)

    def powers(base, n):
        out = [base]
        for _ in range(n - 1):
            out.append(_cmul(out[-1][0], out[-1][1], base[0], base[1]))
        return jnp.stack([jnp.stack(p) for p in out]).reshape(n, 2, n_tiles, S5_TILE_STATES)

    nsub = V7X_SUBLANES
    pw = powers((lb_re, lb_im), nsub)
    pw8 = powers((pw[nsub - 1, 0].reshape(lb_re.shape), pw[nsub - 1, 1].reshape(lb_re.shape)), nsub)
    sub = jnp.arange(nsub)[None, :, None]
    same = jnp.broadcast_to(pw[:, :, :, None, :], (nsub, 2, n_tiles, nsub, S5_TILE_STATES))
    hs = jnp.stack([jnp.where(sub >= s, pw8[s - 1][:, :, None, :], 0.0) for s in S5_STEPS])
    carry = pw8.transpose(1, 2, 0, 3)[None]
    tab = jnp.concatenate([same, hs, carry], axis=0).transpose(2, 0, 1, 3, 4)
    return bw, cw, tab


def _s5_scan(x, gain, bw, cw, tab, d_skip, *, rows):
    bsz, seqlen, d = x.shape
    n_tiles = bw.shape[0]
    two_ns = 2 * S5_TILE_STATES
    return pl.pallas_call(
        functools.partial(_s5_kernel, rows=rows),
        grid=(bsz, seqlen // rows),
        in_specs=[
            pl.BlockSpec((None, rows, d), lambda b, c: (b, c, 0)),
            _const_spec((1, d)),
            _const_spec(bw.shape),
            _const_spec(cw.shape),
            _const_spec(tab.shape),
            _const_spec((1, d)),
        ],
        out_specs=pl.BlockSpec((None, rows, d), lambda b, c: (b, c, 0)),
        out_shape=jax.ShapeDtypeStruct((bsz, seqlen, d), BF16),
        scratch_shapes=[
            pltpu.VMEM((n_tiles, 2, V7X_SUBLANES, S5_TILE_STATES), F32),
            pltpu.VMEM((2, two_ns // V7X_LANES, rows, V7X_LANES), F32),
            pltpu.VMEM((d // V7X_LANES, rows, V7X_LANES), F32),
            pltpu.VMEM((rows, d), BF16),
            pltpu.VMEM((2, V7X_MXU_DIM // V7X_LANES, rows, V7X_LANES), F32),
        ],
        compiler_params=_params(("arbitrary", "arbitrary")),
        name="s5_scan",
    )(x, gain.reshape(1, d), bw, cw, tab, d_skip.reshape(1, d))


def _sigmoid(x):
    return 1.0 / (1.0 + jnp.exp(-x))


def _glu_kernel(g_ref, wv_ref, wg_ref, x_ref, o_ref):
    g = g_ref[...]
    val = jnp.dot(g, wv_ref[...], preferred_element_type=F32)
    gate = jnp.dot(g, wg_ref[...], preferred_element_type=F32)
    o_ref[...] = x_ref[...] + val * _sigmoid(gate)


def _glu_proj(g, w_glu, x, *, tm, tn):
    t, d = x.shape
    nj = d // tn
    return pl.pallas_call(
        _glu_kernel,
        grid=(t // tm, nj),
        in_specs=[
            pl.BlockSpec((tm, d), lambda i, j: (i, 0)),
            pl.BlockSpec((d, tn), lambda i, j: (0, j)),
            pl.BlockSpec((d, tn), lambda i, j: (0, j + nj)),
            pl.BlockSpec((tm, tn), lambda i, j: (i, j)),
        ],
        out_specs=pl.BlockSpec((tm, tn), lambda i, j: (i, j)),
        out_shape=jax.ShapeDtypeStruct((t, d), F32),
        compiler_params=_params(("arbitrary", "arbitrary")),
        name="glu_proj",
    )(g, w_glu, w_glu, x)


def _ffn_kernel(x_ref, gain_ref, wg_ref, wu_ref, wd_ref, o_ref, h_ref, acc_ref):
    j = pl.program_id(1)

    @pl.when(j == 0)
    def _():
        x = x_ref[...]
        h_ref[...] = (x * _rms_scale(x) * gain_ref[...]).astype(BF16)
        acc_ref[...] = jnp.zeros_like(acc_ref)

    h = h_ref[...]
    gate = jnp.dot(h, wg_ref[...], preferred_element_type=F32)
    up = jnp.dot(h, wu_ref[...], preferred_element_type=F32)
    act = (gate * _sigmoid(gate) * up).astype(BF16)
    acc_ref[...] += jnp.dot(act, wd_ref[...], preferred_element_type=F32)

    @pl.when(j == pl.num_programs(1) - 1)
    def _():
        o_ref[...] = x_ref[...] + acc_ref[...]


def _ffn(x, gain, w_gate_up, w_down, *, tm, tf):
    t, d = x.shape
    d_ff = w_down.shape[0]
    nj = d_ff // tf
    return pl.pallas_call(
        _ffn_kernel,
        grid=(t // tm, nj),
        in_specs=[
            pl.BlockSpec((tm, d), lambda i, j: (i, 0)),
            _const_spec((1, d)),
            pl.BlockSpec((d, tf), lambda i, j: (0, j)),
            pl.BlockSpec((d, tf), lambda i, j: (0, j + nj)),
            pl.BlockSpec((tf, d), lambda i, j: (j, 0)),
        ],
        out_specs=pl.BlockSpec((tm, d), lambda i, j: (i, 0)),
        out_shape=jax.ShapeDtypeStruct((t, d), F32),
        scratch_shapes=[pltpu.VMEM((tm, d), BF16), pltpu.VMEM((tm, d), F32)],
        compiler_params=_params(("arbitrary", "arbitrary")),
        name="ffn",
    )(x, gain.reshape(1, d), w_gate_up, w_gate_up, w_down)


ATTN_SPAN = ATTN_BLOCK * max(d for _, d in DILATION_CFG)
SPAN_RES = ATTN_SPAN // ATTN_BLOCK
QKV_HEADS_PER_STEP = 4
QKV_ROWS = ATTN_SPAN // 2
LOG2E = math.log2(math.e)


def _head_norm(y, gain):
    parts = []
    for a in range(y.shape[1] // HEAD_DIM):
        ya = y[:, a * HEAD_DIM:(a + 1) * HEAD_DIM]
        parts.append(ya * _rms_scale(ya))
    return jnp.concatenate(parts, axis=1) * gain


def _store_heads(o_ref, y):
    for a in range(o_ref.shape[0]):
        ya = y[:, a * HEAD_DIM:(a + 1) * HEAD_DIM]
        o_ref[a] = ya.reshape(o_ref.shape[1:]).astype(o_ref.dtype)


def _qkv_kernel(x_ref, gains_ref, w_ref, hgain_ref, q0_ref, q12_ref, kv32_ref, kv16_ref,
                slab_ref, hq_ref, hk_ref, *, n_q0, n_q, n_k):
    j = pl.program_id(2)

    @pl.when(j == 0)
    def _():
        x = x_ref[...]
        xn = x * _rms_scale(x)
        per_res = x.shape[0] // SPAN_RES
        n_slab = slab_ref.shape[0]
        for c in range(n_slab):
            slab_ref[c] = xn[:, c * V7X_LANES:(c + 1) * V7X_LANES]
        for c in range(n_slab):
            lanes = slice(c * V7X_LANES, (c + 1) * V7X_LANES)
            for r in range(SPAN_RES):
                piece = slab_ref[c, pl.ds(r, per_res, stride=SPAN_RES), :]
                rows = slice(r * per_res, (r + 1) * per_res)
                hq_ref[rows, lanes] = (piece * gains_ref[0:1, lanes]).astype(BF16)
                hk_ref[rows, lanes] = (piece * gains_ref[1:2, lanes]).astype(BF16)

    @pl.when(j < n_q0)
    def _():
        y = jnp.dot(hq_ref[...], w_ref[...], preferred_element_type=F32)
        _store_heads(q0_ref, _head_norm(y, hgain_ref[0]))

    @pl.when((j >= n_q0) & (j < n_q))
    def _():
        y = jnp.dot(hq_ref[...], w_ref[...], preferred_element_type=F32)
        _store_heads(q12_ref, _head_norm(y, hgain_ref[0]))

    @pl.when((j >= n_q) & (j < n_q + n_k))
    def _():
        y = _head_norm(jnp.dot(hk_ref[...], w_ref[...], preferred_element_type=F32), hgain_ref[0])
        _store_heads(kv32_ref, y)
        _store_heads(kv16_ref, y)

    @pl.when(j >= n_q + n_k)
    def _():
        y = jnp.dot(hk_ref[...], w_ref[...], preferred_element_type=F32)
        _store_heads(kv32_ref, y)
        _store_heads(kv16_ref, y)


def _qkv_proj(x, gains, w_all, hgain, *, n_heads):
    bsz, seqlen, d = x.shape
    n_span = seqlen // ATTN_SPAN
    hb = QKV_HEADS_PER_STEP
    tn = hb * HEAD_DIM
    n_q0 = n_heads // hb
    n_q = N_GROUPS * n_heads // hb
    n_k = n_heads // hb
    n_steps = w_all.shape[1] // tn
    rows = QKV_ROWS
    tiles_per_span = ATTN_SPAN // rows
    per_res = rows // SPAN_RES

    def head_block(lo, hi):
        return lambda b, u, j: (b, u // tiles_per_span, jnp.clip(j - lo, 0, hi - lo - 1), 0, u % tiles_per_span, 0)

    def out(n_h, dtype):
        return jax.ShapeDtypeStruct((bsz, n_span, n_h, SPAN_RES, ATTN_BLOCK, HEAD_DIM), dtype)

    blk = (None, None, hb, SPAN_RES, per_res, HEAD_DIM)
    return pl.pallas_call(
        functools.partial(_qkv_kernel, n_q0=n_q0, n_q=n_q, n_k=n_k),
        grid=(bsz, seqlen // rows, n_steps),
        in_specs=[
            pl.BlockSpec((None, rows, d), lambda b, u, j: (b, u, 0)),
            _const_spec((2, d)),
            pl.BlockSpec((d, tn), lambda b, u, j: (0, j)),
            pl.BlockSpec((1, 1, tn), lambda b, u, j: (j, 0, 0)),
        ],
        out_specs=[
            pl.BlockSpec(blk, head_block(0, n_q0)),
            pl.BlockSpec(blk, head_block(n_q0, n_q)),
            pl.BlockSpec(blk, head_block(n_q, n_steps)),
            pl.BlockSpec(blk, head_block(n_q, n_steps)),
        ],
        out_shape=[out(n_heads, F32), out((N_GROUPS - 1) * n_heads, BF16),
                   out(2 * n_heads, F32), out(2 * n_heads, BF16)],
        scratch_shapes=[pltpu.VMEM((d // V7X_LANES, rows, V7X_LANES), F32),
                        pltpu.VMEM((rows, d), BF16), pltpu.VMEM((rows, d), BF16)],
        compiler_params=_params(("arbitrary", "arbitrary", "arbitrary")),
        name="qkv_proj",
    )(x, gains, w_all, hgain)


def _attend(q, kk, vv, bias):
    s = lax.dot_general(q, kk, (((1,), (1,)), ((), ())), preferred_element_type=F32) + bias
    m = jnp.max(s, axis=-1, keepdims=True)
    p = jnp.exp2(s - m)
    l = jnp.sum(p, axis=-1, keepdims=True)
    o = jnp.dot(p.astype(BF16), vv, preferred_element_type=F32) * (1.0 / l)
    return o, m + jnp.log2(l)


def _attn_kernel(slope_ref, q0_ref, q1_ref, q2_ref, k32_ref, v32_ref, k32p_ref, v32p_ref,
                 k16_ref, v16_ref, k16p_ref, v16p_ref, o_ref, og_ref, lg_ref):
    s_idx = pl.program_id(1)
    head = pl.program_id(2)
    blk = ATTN_BLOCK
    row = lax.broadcasted_iota(jnp.int32, (blk, 2 * blk), 0)
    col = lax.broadcasted_iota(jnp.int32, (blk, 2 * blk), 1)
    in_prev = col < blk
    no_prev = jnp.where(in_prev, jnp.where(s_idx == 0, NEG_BIG, 0.0).astype(F32), 0.0)

    sources = ((q0_ref, k32_ref, v32_ref, k32p_ref, v32p_ref),
               (q1_ref, k16_ref, v16_ref, k16p_ref, v16p_ref),
               (q2_ref, k16_ref, v16_ref, k16p_ref, v16p_ref))
    for g, ((window, dil), (q_ref, k_ref, v_ref, kp_ref, vp_ref)) in enumerate(zip(DILATION_CFG, sources)):
        n_pieces = SPAN_RES // dil
        piece_rows = blk // n_pieces
        shift = piece_rows.bit_length() - 1
        prev_lo = kp_ref.shape[1] - piece_rows

        def pos(x, n_pieces=n_pieces, piece_rows=piece_rows, shift=shift):
            return jnp.right_shift(x, shift) + n_pieces * jnp.bitwise_and(x, piece_rows - 1)

        dist = (blk + pos(row)) - (pos(jnp.bitwise_and(col, blk - 1)) + jnp.where(in_prev, 0, blk))
        coef = slope_ref[g, head] * (-dil * LOG2E)
        bias = jnp.where((dist >= 0) & (dist <= window // dil), dist.astype(F32) * coef, NEG_BIG)
        bias_first = bias + no_prev

        for rd in range(dil):
            rs = [rd + dil * j for j in range(n_pieces)]
            for n in range(n_pieces):
                lo = n * piece_rows

                def gather(cur_ref, prev_ref, n=n, lo=lo, rs=rs):
                    cur = [cur_ref[r, lo:lo + piece_rows, :] for r in rs]
                    if n == 0:
                        prev = [prev_ref[r, prev_lo:prev_lo + piece_rows, :] for r in rs]
                    else:
                        prev = [cur_ref[r, lo - piece_rows:lo, :] for r in rs]
                    return jnp.concatenate(prev + cur, axis=0).astype(BF16)

                q = jnp.concatenate([q_ref[r, lo:lo + piece_rows, :] for r in rs], axis=0).astype(BF16)
                o, lse = _attend(q, gather(k_ref, kp_ref), gather(v_ref, vp_ref),
                                 bias_first if n == 0 else bias)
                lse = jnp.broadcast_to(lse, (blk, HEAD_DIM))
                for j, r in enumerate(rs):
                    og_ref[g, r, lo:lo + piece_rows, :] = o[j * piece_rows:(j + 1) * piece_rows]
                    lg_ref[g, r, lo:lo + piece_rows, :] = lse[j * piece_rows:(j + 1) * piece_rows]

    lses = [lg_ref[g] for g in range(N_GROUPS)]
    top = functools.reduce(jnp.maximum, lses)
    wts = [jnp.exp2(l - top) for l in lses]
    num = sum(w * og_ref[g] for g, w in enumerate(wts))
    merged = num / sum(wts)
    for r in range(SPAN_RES):
        o_ref[pl.ds(r, blk, stride=SPAN_RES), :] = merged[r]


def _dilated_attn(q0, q12, kv32, kv16, slopes):
    bsz, n_span, n_heads = q0.shape[:3]
    full = (None, None, None, SPAN_RES, ATTN_BLOCK, HEAD_DIM)
    tail = (None, None, None, SPAN_RES, V7X_SUBLANES, HEAD_DIM)
    last_tail = ATTN_BLOCK // V7X_SUBLANES - 1

    def cur(base):
        return lambda b, s, h: (b, s, base + h, 0, 0, 0)

    def prev(base, row_block=0):
        return lambda b, s, h: (b, jnp.maximum(s - 1, 0), base + h, 0, row_block, 0)

    return pl.pallas_call(
        _attn_kernel,
        grid=(bsz, n_span, n_heads),
        in_specs=[
            pl.BlockSpec(memory_space=pltpu.SMEM),
            pl.BlockSpec(full, cur(0)),
            pl.BlockSpec(full, cur(0)),
            pl.BlockSpec(full, cur(n_heads)),
            pl.BlockSpec(full, cur(0)),
            pl.BlockSpec(full, cur(n_heads)),
            pl.BlockSpec(tail, prev(0, last_tail)),
            pl.BlockSpec(tail, prev(n_heads, last_tail)),
            pl.BlockSpec(full, cur(0)),
            pl.BlockSpec(full, cur(n_heads)),
            pl.BlockSpec(full, prev(0)),
            pl.BlockSpec(full, prev(n_heads)),
        ],
        out_specs=pl.BlockSpec((None, ATTN_SPAN, HEAD_DIM), lambda b, s, h: (b, s, h)),
        out_shape=jax.ShapeDtypeStruct((bsz, n_span * ATTN_SPAN, n_heads * HEAD_DIM), F32),
        scratch_shapes=[
            pltpu.VMEM((N_GROUPS, SPAN_RES, ATTN_BLOCK, HEAD_DIM), F32),
            pltpu.VMEM((N_GROUPS, SPAN_RES, ATTN_BLOCK, HEAD_DIM), F32),
        ],
        compiler_params=_params(("arbitrary", "arbitrary", "arbitrary")),
        name="dilated_attn",
    )(slopes, q0, q12, q12, kv32, kv32, kv32, kv32, kv16, kv16, kv16, kv16)


def _out_kernel(a_ref, w_ref, x_ref, o_ref):
    a = a_ref[...].astype(BF16)
    o_ref[...] = x_ref[...] + jnp.dot(a, w_ref[...], preferred_element_type=F32)


def _out_proj(a, w, x, *, tm, tn):
    t, d = x.shape
    k = a.shape[1]
    return pl.pallas_call(
        _out_kernel,
        grid=(t // tm, d // tn),
        in_specs=[
            pl.BlockSpec((tm, k), lambda i, j: (i, 0)),
            pl.BlockSpec((k, tn), lambda i, j: (0, j)),
            pl.BlockSpec((tm, tn), lambda i, j: (i, j)),
        ],
        out_specs=pl.BlockSpec((tm, tn), lambda i, j: (i, j)),
        out_shape=jax.ShapeDtypeStruct((t, d), F32),
        compiler_params=_params(("arbitrary", "arbitrary")),
        name="out_proj",
    )(a, w, x)


def _alibi_slopes(n_heads):
    n = N_GROUPS * n_heads
    i = jnp.arange(1, n + 1, dtype=F32)
    return jnp.exp2(-8.0 * i / n).reshape(n_heads, N_GROUPS).T


def kernel(x, mix_norm, ffn_norm, ssm_a_re, ssm_a_im, ssm_log_dt, ssm_b_re, ssm_b_im, ssm_c_re, ssm_c_im,
           ssm_d, w_glu, kv_norm, w_kv, k_norm, w_q, q_norm, w_o, w_gate_up, w_down):
    bsz, seqlen, d = x.shape
    t = bsz * seqlen
    n_heads = w_o.shape[1] // HEAD_DIM
    assert ssm_a_re.shape[0] == 1 and w_q.shape[0] == 1, "one S5 layer followed by one attention layer"
    assert seqlen % ATTN_SPAN == 0

    bw, cw, tab = _s5_prepare(ssm_a_re[0], ssm_a_im[0], ssm_log_dt[0], ssm_b_re[0], ssm_b_im[0],
                              ssm_c_re[0], ssm_c_im[0])
    g = _s5_scan(x, mix_norm[0], bw, cw, tab, ssm_d[0], rows=128)
    x = x.reshape(t, d)
    x = _glu_proj(g.reshape(t, d), w_glu[0].astype(BF16), x, tm=1024, tn=512)
    x = _ffn(x, ffn_norm[0], w_gate_up[0].astype(BF16), w_down[0].astype(BF16), tm=512, tf=512)

    w_all = jnp.concatenate([w_q[0], w_kv], axis=1).astype(BF16)
    q_gain = jnp.repeat(q_norm[0].astype(F32) * (HEAD_DIM ** -0.5 * LOG2E), n_heads, axis=0)
    k_gain = jnp.tile(k_norm.astype(F32)[None], (n_heads, 1))
    v_gain = jnp.ones((n_heads, HEAD_DIM), F32)
    hgain = jnp.concatenate([q_gain, k_gain, v_gain], axis=0).reshape(-1, 1, QKV_HEADS_PER_STEP * HEAD_DIM)
    gains = jnp.stack([mix_norm[1], kv_norm]).astype(F32)
    q0, q12, kv32, kv16 = _qkv_proj(x.reshape(bsz, seqlen, d), gains, w_all, hgain, n_heads=n_heads)
    merged = _dilated_attn(q0, q12, kv32, kv16, _alibi_slopes(n_heads))
    x = _out_proj(merged.reshape(t, n_heads * HEAD_DIM), w_o[0].astype(BF16), x, tm=512, tn=1024)
    x = _ffn(x, ffn_norm[1], w_gate_up[1].astype(BF16), w_down[1].astype(BF16), tm=512, tf=512)
    return x.reshape(bsz, seqlen, d)
```

```python
import functools
import math

import jax
import jax.numpy as jnp
from jax import lax
from jax.experimental import pallas as pl
from jax.experimental.pallas import tpu as pltpu

F32 = jnp.float32
BF16 = jnp.bfloat16

EPS = 1e-6
SSM_GROUP = 16
SSM_STATE = 64
HEAD_DIM = 128
ATTN_BLOCK = 128
DILATION_CFG = ((128, 1), (512, 4), (2048, 16))
N_GROUPS = len(DILATION_CFG)
NEG_BIG = -1e30

V7X_SUBLANES = 8
V7X_LANES = 128
V7X_MXU_DIM = 256
V7X_VMEM_BYTES = 64 * 1024 * 1024
VMEM_LIMIT = V7X_VMEM_BYTES - 8 * 1024 * 1024


def _params(semantics):
    return pltpu.CompilerParams(dimension_semantics=semantics, vmem_limit_bytes=VMEM_LIMIT)


def _const_spec(shape):
    nd = len(shape)
    return pl.BlockSpec(shape, lambda *_: (0,) * nd, pipeline_mode=pl.Buffered(1))


def _rms_scale(x):
    return lax.rsqrt(jnp.mean(x * x, axis=-1, keepdims=True) + EPS)


S5_TILE_GROUPS = V7X_MXU_DIM // SSM_GROUP
S5_TILE_STATES = S5_TILE_GROUPS * SSM_STATE
S5_STEPS = (1, 2, 4)
S5_SUPER = V7X_SUBLANES * V7X_SUBLANES
S5_TAB_POW = 0
S5_TAB_HS = V7X_SUBLANES
S5_TAB_CARRY = S5_TAB_HS + len(S5_STEPS)
S5_N_TABS = S5_TAB_CARRY + 1


def _cmul(ar, ai, br, bi):
    return ar * br - ai * bi, ar * bi + ai * br


def _s5_scan_column(slab_ref, tab_ref, m, c, row0, h):
    n_col = slab_ref.shape[0] // 2
    lanes = slice(c * V7X_LANES, (c + 1) * V7X_LANES)

    def tab(t):
        return tab_ref[m, t, 0, :, lanes], tab_ref[m, t, 1, :, lanes]

    def rows(k):
        return pl.ds(row0 + k * V7X_SUBLANES, V7X_SUBLANES)

    lam = tab(S5_TAB_POW)
    loc = []
    for k in range(V7X_SUBLANES):
        xr = slab_ref[c, rows(k), :]
        xi = slab_ref[n_col + c, rows(k), :]
        if k:
            pr, pi = _cmul(lam[0], lam[1], loc[-1][0], loc[-1][1])
            xr, xi = xr + pr, xi + pi
        loc.append((xr, xi))

    er, ei = loc[-1]
    for t, s in enumerate(S5_STEPS):
        ar, ai = tab(S5_TAB_HS + t)
        pr, pi = _cmul(ar, ai, pltpu.roll(er, s, 0), pltpu.roll(ei, s, 0))
        er, ei = er + pr, ei + pi
    cr, ci = tab(S5_TAB_CARRY)
    pr, pi = _cmul(cr, ci, h[0], h[1])
    er, ei = er + pr, ei + pi
    first = lax.broadcasted_iota(jnp.int32, er.shape, 0) == 0
    inr = jnp.where(first, h[0], pltpu.roll(er, 1, 0))
    ini = jnp.where(first, h[1], pltpu.roll(ei, 1, 0))

    for k in range(V7X_SUBLANES):
        qr, qi = tab(S5_TAB_POW + k)
        pr, pi = _cmul(qr, qi, inr, ini)
        slab_ref[c, rows(k), :] = loc[k][0] + pr
        slab_ref[n_col + c, rows(k), :] = loc[k][1] + pi

    last = V7X_SUBLANES - 1
    return (jnp.broadcast_to(er[last:, :], er.shape), jnp.broadcast_to(ei[last:, :], ei.shape))


def _tile_transpose(ref, rows):
    return jnp.concatenate([ref[pl.ds(row0 + k, V7X_SUBLANES, stride=V7X_SUBLANES), :]
                            for row0 in range(0, rows, S5_SUPER) for k in range(V7X_SUBLANES)], axis=0)


def _sigmoid(x):
    return 1.0 / (1.0 + jnp.exp(-x))


def _s5_kernel(x_ref, xprev_ref, gain_ref, bw_ref, cw_ref, tab_ref, dskip_ref, wglu_ref, o_ref,
               h_ref, slabs_ref, perm_ref, ubf_ref, yperm_ref, gnew_ref, gold_ref, *, rows):
    n_tiles = bw_ref.shape[0]
    n_slab = slabs_ref.shape[1]
    n_col = n_slab // 2
    d = x_ref.shape[1]

    @pl.when(pl.program_id(1) == 0)
    def _():
        h_ref[...] = jnp.zeros_like(h_ref)
        gold_ref[...] = jnp.zeros_like(gold_ref)

    g_prev = gold_ref[...]
    for j in range(d // V7X_MXU_DIM):
        cols = slice(j * V7X_MXU_DIM, (j + 1) * V7X_MXU_DIM)
        val = jnp.dot(g_prev, wglu_ref[:, cols], preferred_element_type=F32)
        gate = jnp.dot(g_prev, wglu_ref[:, d + j * V7X_MXU_DIM:d + (j + 1) * V7X_MXU_DIM],
                       preferred_element_type=F32)
        o_ref[:, cols] = xprev_ref[:, cols] + val * _sigmoid(gate)

    x = x_ref[...]
    u = x * _rms_scale(x) * gain_ref[...]
    for s in range(perm_ref.shape[0]):
        perm_ref[s] = u[:, s * V7X_LANES:(s + 1) * V7X_LANES]
    for s in range(perm_ref.shape[0]):
        ubf_ref[:, s * V7X_LANES:(s + 1) * V7X_LANES] = _tile_transpose(perm_ref.at[s], rows).astype(BF16)

    for m in range(n_tiles):
        lo = m * V7X_MXU_DIM
        hi = lo + V7X_MXU_DIM
        slab_ref = slabs_ref.at[m % 2]
        bu = jnp.dot(ubf_ref[:, lo:hi], bw_ref[m], preferred_element_type=F32)
        for s in range(n_slab):
            slab_ref[s] = bu[:, s * V7X_LANES:(s + 1) * V7X_LANES]

        for c in range(n_col):
            h = (h_ref[m, 0, :, c * V7X_LANES:(c + 1) * V7X_LANES],
                 h_ref[m, 1, :, c * V7X_LANES:(c + 1) * V7X_LANES])
            for row0 in range(0, rows, S5_SUPER):
                h = _s5_scan_column(slab_ref, tab_ref, m, c, row0, h)
            h_ref[m, 0, :, c * V7X_LANES:(c + 1) * V7X_LANES] = h[0]
            h_ref[m, 1, :, c * V7X_LANES:(c + 1) * V7X_LANES] = h[1]

        st = jnp.concatenate([slab_ref[s] for s in range(n_slab)], axis=1).astype(BF16)
        y = jnp.dot(st, cw_ref[m], preferred_element_type=F32)
        y_ref = yperm_ref.at[m % 2]
        for s in range(y_ref.shape[0]):
            y_ref[s] = y[:, s * V7X_LANES:(s + 1) * V7X_LANES]
        y = jnp.concatenate([_tile_transpose(y_ref.at[s], rows) for s in range(y_ref.shape[0])], axis=1)
        y = y + dskip_ref[:, lo:hi] * u[:, lo:hi]
        gnew_ref[:, lo:hi] = jax.nn.gelu(y).astype(BF16)

    gold_ref[...] = gnew_ref[...]


def _s5_prepare(a_re, a_im, log_dt, b_re, b_im, c_re, c_im):
    lam_re = jnp.minimum(a_re.astype(F32), -1e-4)
    lam_im = a_im.astype(F32)
    dt = jnp.exp(log_dt.astype(F32))[:, None]
    mag = jnp.exp(lam_re * dt)
    ang = lam_im * dt
    lb_re = mag * jnp.cos(ang)
    lb_im = mag * jnp.sin(ang)
    den = lam_re * lam_re + lam_im * lam_im
    nr = lb_re - 1.0
    ni = lb_im
    coef_re = (nr * lam_re + ni * lam_im) / den
    coef_im = (ni * lam_re - nr * lam_im) / den
    br = b_re.astype(F32)
    bi = b_im.astype(F32)
    bb_re = coef_re[..., None] * br - coef_im[..., None] * bi
    bb_im = coef_re[..., None] * bi + coef_im[..., None] * br

    n_groups = a_re.shape[0]
    n_tiles = n_groups // S5_TILE_GROUPS
    eye = jnp.eye(S5_TILE_GROUPS, dtype=F32)

    def pack_b(w):
        w = w.reshape(n_tiles, S5_TILE_GROUPS, SSM_STATE, SSM_GROUP)
        return jnp.einsum('mgph,gk->mghkp', w, eye).reshape(n_tiles, V7X_MXU_DIM, S5_TILE_STATES)

    def pack_c(w):
        w = w.reshape(n_tiles, S5_TILE_GROUPS, SSM_GROUP, SSM_STATE)
        return jnp.einsum('mghp,gk->mgpkh', w, eye).reshape(n_tiles, S5_TILE_STATES, V7X_MXU_DIM)

    bw = jnp.concatenate([pack_b(bb_re), pack_b(bb_im)], axis=-1).astype(BF16)
    cw = jnp.concatenate([pack_c(c_re.astype(F32)), -pack_c(c_im.astype(F32))], axis=1).astype(BF16)

    def powers(base, n):
        out = [base]
        for _ in range(n - 1):
            out.append(_cmul(out[-1][0], out[-1][1], base[0], base[1]))
        return jnp.stack([jnp.stack(p) for p in out]).reshape(n, 2, n_tiles, S5_TILE_STATES)

    nsub = V7X_SUBLANES
    pw = powers((lb_re, lb_im), nsub)
    pw8 = powers((pw[nsub - 1, 0].reshape(lb_re.shape), pw[nsub - 1, 1].reshape(lb_re.shape)), nsub)
    sub = jnp.arange(nsub)[None, :, None]
    same = jnp.broadcast_to(pw[:, :, :, None, :], (nsub, 2, n_tiles, nsub, S5_TILE_STATES))
    hs = jnp.stack([jnp.where(sub >= s, pw8[s - 1][:, :, None, :], 0.0) for s in S5_STEPS])
    carry = pw8.transpose(1, 2, 0, 3)[None]
    tab = jnp.concatenate([same, hs, carry], axis=0).transpose(2, 0, 1, 3, 4)
    return bw, cw, tab


def _s5_mixer(x, gain, bw, cw, tab, d_skip, w_glu, *, rows):
    bsz, seqlen, d = x.shape
    n_tiles = bw.shape[0]
    n_blocks = seqlen // rows
    two_ns = 2 * S5_TILE_STATES
    lagged = pl.BlockSpec((None, rows, d), lambda b, c: (b, jnp.maximum(c - 1, 0), 0))
    return pl.pallas_call(
        functools.partial(_s5_kernel, rows=rows),
        grid=(bsz, n_blocks + 1),
        in_specs=[
            pl.BlockSpec((None, rows, d), lambda b, c: (b, jnp.minimum(c, n_blocks - 1), 0)),
            lagged,
            _const_spec((1, d)),
            _const_spec(bw.shape),
            _const_spec(cw.shape),
            _const_spec(tab.shape),
            _const_spec((1, d)),
            _const_spec(w_glu.shape),
        ],
        out_specs=lagged,
        out_shape=jax.ShapeDtypeStruct((bsz, seqlen, d), F32),
        scratch_shapes=[
            pltpu.VMEM((n_tiles, 2, V7X_SUBLANES, S5_TILE_STATES), F32),
            pltpu.VMEM((2, two_ns // V7X_LANES, rows, V7X_LANES), F32),
            pltpu.VMEM((d // V7X_LANES, rows, V7X_LANES), F32),
            pltpu.VMEM((rows, d), BF16),
            pltpu.VMEM((2, V7X_MXU_DIM // V7X_LANES, rows, V7X_LANES), F32),
            pltpu.VMEM((rows, d), BF16),
            pltpu.VMEM((rows, d), BF16),
        ],
        compiler_params=_params(("arbitrary", "arbitrary")),
        name="s5_mixer",
    )(x, x, gain.reshape(1, d), bw, cw, tab, d_skip.reshape(1, d), w_glu)


def _ffn_kernel(x_ref, gain_ref, wg_ref, wu_ref, wd_ref, o_ref, h_ref, acc_ref):
    j = pl.program_id(1)

    @pl.when(j == 0)
    def _():
        x = x_ref[...]
        h_ref[...] = (x * _rms_scale(x) * gain_ref[...]).astype(BF16)
        acc_ref[...] = jnp.zeros_like(acc_ref)

    h = h_ref[...]
    gate = jnp.dot(h, wg_ref[...], preferred_element_type=F32)
    up = jnp.dot(h, wu_ref[...], preferred_element_type=F32)
    act = (gate * _sigmoid(gate) * up).astype(BF16)
    acc_ref[...] += jnp.dot(act, wd_ref[...], preferred_element_type=F32)

    @pl.when(j == pl.num_programs(1) - 1)
    def _():
        o_ref[...] = x_ref[...] + acc_ref[...]


def _ffn(x, gain, w_gate_up, w_down, layer, *, tm, tf):
    t, d = x.shape
    d_ff = w_down.shape[1]
    nj = d_ff // tf
    return pl.pallas_call(
        _ffn_kernel,
        grid=(t // tm, nj),
        in_specs=[
            pl.BlockSpec((tm, d), lambda i, j: (i, 0)),
            _const_spec((1, d)),
            pl.BlockSpec((None, d, tf), lambda i, j: (layer, 0, j)),
            pl.BlockSpec((None, d, tf), lambda i, j: (layer, 0, j + nj)),
            pl.BlockSpec((None, tf, d), lambda i, j: (layer, j, 0)),
        ],
        out_specs=pl.BlockSpec((tm, d), lambda i, j: (i, 0)),
        out_shape=jax.ShapeDtypeStruct((t, d), F32),
        scratch_shapes=[pltpu.VMEM((tm, d), BF16), pltpu.VMEM((tm, d), F32)],
        compiler_params=_params(("arbitrary", "arbitrary")),
        name="ffn",
    )(x, gain.reshape(1, d), w_gate_up, w_gate_up, w_down)


ATTN_SPAN = ATTN_BLOCK * max(d for _, d in DILATION_CFG)
SPAN_RES = ATTN_SPAN // ATTN_BLOCK
QKV_HEADS_PER_STEP = 4
QKV_ROWS = ATTN_SPAN // 2
LOG2E = math.log2(math.e)


def _head_norm(y, gain):
    parts = []
    for a in range(y.shape[1] // HEAD_DIM):
        ya = y[:, a * HEAD_DIM:(a + 1) * HEAD_DIM]
        parts.append(ya * _rms_scale(ya))
    return jnp.concatenate(parts, axis=1) * gain


def _store_heads(o_ref, y):
    for a in range(o_ref.shape[0]):
        ya = y[:, a * HEAD_DIM:(a + 1) * HEAD_DIM]
        o_ref[a] = ya.reshape(o_ref.shape[1:]).astype(o_ref.dtype)


def _qkv_kernel(x_ref, gains_ref, w_ref, hgain_ref, q0_ref, q12_ref, kv32_ref, kv16_ref,
                slab_ref, hq_ref, hk_ref, *, n_q0, n_q, n_k):
    j = pl.program_id(2)

    @pl.when(j == 0)
    def _():
        x = x_ref[...]
        xn = x * _rms_scale(x)
        per_res = x.shape[0] // SPAN_RES
        n_slab = slab_ref.shape[0]
        for c in range(n_slab):
            slab_ref[c] = xn[:, c * V7X_LANES:(c + 1) * V7X_LANES]
        for c in range(n_slab):
            lanes = slice(c * V7X_LANES, (c + 1) * V7X_LANES)
            for r in range(SPAN_RES):
                piece = slab_ref[c, pl.ds(r, per_res, stride=SPAN_RES), :]
                rows = slice(r * per_res, (r + 1) * per_res)
                hq_ref[rows, lanes] = (piece * gains_ref[0:1, lanes]).astype(BF16)
                hk_ref[rows, lanes] = (piece * gains_ref[1:2, lanes]).astype(BF16)

    @pl.when(j < n_q0)
    def _():
        y = jnp.dot(hq_ref[...], w_ref[...], preferred_element_type=F32)
        _store_heads(q0_ref, _head_norm(y, hgain_ref[0]))

    @pl.when((j >= n_q0) & (j < n_q))
    def _():
        y = jnp.dot(hq_ref[...], w_ref[...], preferred_element_type=F32)
        _store_heads(q12_ref, _head_norm(y, hgain_ref[0]))

    @pl.when((j >= n_q) & (j < n_q + n_k))
    def _():
        y = _head_norm(jnp.dot(hk_ref[...], w_ref[...], preferred_element_type=F32), hgain_ref[0])
        _store_heads(kv32_ref, y)
        _store_heads(kv16_ref, y)

    @pl.when(j >= n_q + n_k)
    def _():
        y = jnp.dot(hk_ref[...], w_ref[...], preferred_element_type=F32)
        _store_heads(kv32_ref, y)
        _store_heads(kv16_ref, y)


def _qkv_proj(x, gains, w_all, hgain, *, n_heads):
    bsz, seqlen, d = x.shape
    n_span = seqlen // ATTN_SPAN
    hb = QKV_HEADS_PER_STEP
    tn = hb * HEAD_DIM
    n_q0 = n_heads // hb
    n_q = N_GROUPS * n_heads // hb
    n_k = n_heads // hb
    n_steps = w_all.shape[1] // tn
    rows = QKV_ROWS
    tiles_per_span = ATTN_SPAN // rows
    per_res = rows // SPAN_RES

    def head_block(lo, hi):
        return lambda b, u, j: (b, u // tiles_per_span, jnp.clip(j - lo, 0, hi - lo - 1), 0, u % tiles_per_span, 0)

    def out(n_h, dtype):
        return jax.ShapeDtypeStruct((bsz, n_span, n_h, SPAN_RES, ATTN_BLOCK, HEAD_DIM), dtype)

    blk = (None, None, hb, SPAN_RES, per_res, HEAD_DIM)
    return pl.pallas_call(
        functools.partial(_qkv_kernel, n_q0=n_q0, n_q=n_q, n_k=n_k),
        grid=(bsz, seqlen // rows, n_steps),
        in_specs=[
            pl.BlockSpec((None, rows, d), lambda b, u, j: (b, u, 0)),
            _const_spec((2, d)),
            pl.BlockSpec((d, tn), lambda b, u, j: (0, j)),
            pl.BlockSpec((1, 1, tn), lambda b, u, j: (j, 0, 0)),
        ],
        out_specs=[
            pl.BlockSpec(blk, head_block(0, n_q0)),
            pl.BlockSpec(blk, head_block(n_q0, n_q)),
            pl.BlockSpec(blk, head_block(n_q, n_steps)),
            pl.BlockSpec(blk, head_block(n_q, n_steps)),
        ],
        out_shape=[out(n_heads, F32), out((N_GROUPS - 1) * n_heads, BF16),
                   out(2 * n_heads, F32), out(2 * n_heads, BF16)],
        scratch_shapes=[pltpu.VMEM((d // V7X_LANES, rows, V7X_LANES), F32),
                        pltpu.VMEM((rows, d), BF16), pltpu.VMEM((rows, d), BF16)],
        compiler_params=_params(("arbitrary", "arbitrary", "arbitrary")),
        name="qkv_proj",
    )(x, gains, w_all, hgain)


def _attend(q, kk, vv, bias):
    s = lax.dot_general(q, kk, (((1,), (1,)), ((), ())), preferred_element_type=F32) + bias
    m = jnp.max(s, axis=-1, keepdims=True)
    p = jnp.exp2(s - m)
    l = jnp.sum(p, axis=-1, keepdims=True)
    o = jnp.dot(p.astype(BF16), vv, preferred_element_type=F32) * (1.0 / l)
    return o, m + jnp.log2(l)


def _attn_kernel(slope_ref, q0_ref, q1_ref, q2_ref, k32_ref, v32_ref, k32p_ref, v32p_ref,
                 k16_ref, v16_ref, k16p_ref, v16p_ref, o_ref, og_ref, lg_ref):
    s_idx = pl.program_id(1)
    head = pl.program_id(2)
    blk = ATTN_BLOCK
    row = lax.broadcasted_iota(jnp.int32, (blk, 2 * blk), 0)
    col = lax.broadcasted_iota(jnp.int32, (blk, 2 * blk), 1)
    in_prev = col < blk
    no_prev = jnp.where(in_prev, jnp.where(s_idx == 0, NEG_BIG, 0.0).astype(F32), 0.0)

    sources = ((q0_ref, k32_ref, v32_ref, k32p_ref, v32p_ref),
               (q1_ref, k16_ref, v16_ref, k16p_ref, v16p_ref),
               (q2_ref, k16_ref, v16_ref, k16p_ref, v16p_ref))
    for g, ((window, dil), (q_ref, k_ref, v_ref, kp_ref, vp_ref)) in enumerate(zip(DILATION_CFG, sources)):
        n_pieces = SPAN_RES // dil
        piece_rows = blk // n_pieces
        shift = piece_rows.bit_length() - 1
        prev_lo = kp_ref.shape[1] - piece_rows

        def pos(x, n_pieces=n_pieces, piece_rows=piece_rows, shift=shift):
            return jnp.right_shift(x, shift) + n_pieces * jnp.bitwise_and(x, piece_rows - 1)

        dist = (blk + pos(row)) - (pos(jnp.bitwise_and(col, blk - 1)) + jnp.where(in_prev, 0, blk))
        coef = slope_ref[g, head] * (-dil * LOG2E)
        bias = jnp.where((dist >= 0) & (dist <= window // dil), dist.astype(F32) * coef, NEG_BIG)
        bias_first = bias + no_prev

        for rd in range(dil):
            rs = [rd + dil * j for j in range(n_pieces)]
            for n in range(n_pieces):
                lo = n * piece_rows

                def gather(cur_ref, prev_ref, n=n, lo=lo, rs=rs):
                    cur = [cur_ref[r, lo:lo + piece_rows, :] for r in rs]
                    if n == 0:
                        prev = [prev_ref[r, prev_lo:prev_lo + piece_rows, :] for r in rs]
                    else:
                        prev = [cur_ref[r, lo - piece_rows:lo, :] for r in rs]
                    return jnp.concatenate(prev + cur, axis=0).astype(BF16)

                q = jnp.concatenate([q_ref[r, lo:lo + piece_rows, :] for r in rs], axis=0).astype(BF16)
                o, lse = _attend(q, gather(k_ref, kp_ref), gather(v_ref, vp_ref),
                                 bias_first if n == 0 else bias)
                lse = jnp.broadcast_to(lse, (blk, HEAD_DIM))
                for j, r in enumerate(rs):
                    og_ref[g, r, lo:lo + piece_rows, :] = o[j * piece_rows:(j + 1) * piece_rows]
                    lg_ref[g, r, lo:lo + piece_rows, :] = lse[j * piece_rows:(j + 1) * piece_rows]

    lses = [lg_ref[g] for g in range(N_GROUPS)]
    top = functools.reduce(jnp.maximum, lses)
    wts = [jnp.exp2(l - top) for l in lses]
    num = sum(w * og_ref[g] for g, w in enumerate(wts))
    merged = num / sum(wts)
    for r in range(SPAN_RES):
        o_ref[pl.ds(r, blk, stride=SPAN_RES), :] = merged[r]


def _dilated_attn(q0, q12, kv32, kv16, slopes):
    bsz, n_span, n_heads = q0.shape[:3]
    full = (None, None, None, SPAN_RES, ATTN_BLOCK, HEAD_DIM)
    tail = (None, None, None, SPAN_RES, V7X_SUBLANES, HEAD_DIM)
    last_tail = ATTN_BLOCK // V7X_SUBLANES - 1

    def cur(base):
        return lambda b, s, h: (b, s, base + h, 0, 0, 0)

    def prev(base, row_block=0):
        return lambda b, s, h: (b, jnp.maximum(s - 1, 0), base + h, 0, row_block, 0)

    return pl.pallas_call(
        _attn_kernel,
        grid=(bsz, n_span, n_heads),
        in_specs=[
            pl.BlockSpec(memory_space=pltpu.SMEM),
            pl.BlockSpec(full, cur(0)),
            pl.BlockSpec(full, cur(0)),
            pl.BlockSpec(full, cur(n_heads)),
            pl.BlockSpec(full, cur(0)),
            pl.BlockSpec(full, cur(n_heads)),
            pl.BlockSpec(tail, prev(0, last_tail)),
            pl.BlockSpec(tail, prev(n_heads, last_tail)),
            pl.BlockSpec(full, cur(0)),
            pl.BlockSpec(full, cur(n_heads)),
            pl.BlockSpec(full, prev(0)),
            pl.BlockSpec(full, prev(n_heads)),
        ],
        out_specs=pl.BlockSpec((None, ATTN_SPAN, HEAD_DIM), lambda b, s, h: (b, s, h)),
        out_shape=jax.ShapeDtypeStruct((bsz, n_span * ATTN_SPAN, n_heads * HEAD_DIM), F32),
        scratch_shapes=[
            pltpu.VMEM((N_GROUPS, SPAN_RES, ATTN_BLOCK, HEAD_DIM), F32),
            pltpu.VMEM((N_GROUPS, SPAN_RES, ATTN_BLOCK, HEAD_DIM), F32),
        ],
        compiler_params=_params(("arbitrary", "arbitrary", "arbitrary")),
        name="dilated_attn",
    )(slopes, q0, q12, q12, kv32, kv32, kv32, kv32, kv16, kv16, kv16, kv16)


def _out_kernel(a_ref, w_ref, x_ref, o_ref):
    a = a_ref[...].astype(BF16)
    o_ref[...] = x_ref[...] + jnp.dot(a, w_ref[...], preferred_element_type=F32)


def _out_proj(a, w, x, *, tm):
    t, d = x.shape
    k = a.shape[1]
    return pl.pallas_call(
        _out_kernel,
        grid=(t // tm,),
        in_specs=[
            pl.BlockSpec((tm, k), lambda i: (i, 0)),
            _const_spec((k, d)),
            pl.BlockSpec((tm, d), lambda i: (i, 0)),
        ],
        out_specs=pl.BlockSpec((tm, d), lambda i: (i, 0)),
        out_shape=jax.ShapeDtypeStruct((t, d), F32),
        compiler_params=_params(("arbitrary",)),
        name="out_proj",
    )(a, w, x)


def _alibi_slopes(n_heads):
    n = N_GROUPS * n_heads
    i = jnp.arange(1, n + 1, dtype=F32)
    return jnp.exp2(-8.0 * i / n).reshape(n_heads, N_GROUPS).T


def kernel(x, mix_norm, ffn_norm, ssm_a_re, ssm_a_im, ssm_log_dt, ssm_b_re, ssm_b_im, ssm_c_re, ssm_c_im,
           ssm_d, w_glu, kv_norm, w_kv, k_norm, w_q, q_norm, w_o, w_gate_up, w_down):
    bsz, seqlen, d = x.shape
    t = bsz * seqlen
    n_heads = w_o.shape[1] // HEAD_DIM
    assert ssm_a_re.shape[0] == 1 and w_q.shape[0] == 1, "one S5 layer followed by one attention layer"
    assert seqlen % ATTN_SPAN == 0

    bw, cw, tab = _s5_prepare(ssm_a_re[0], ssm_a_im[0], ssm_log_dt[0], ssm_b_re[0], ssm_b_im[0],
                              ssm_c_re[0], ssm_c_im[0])
    x = _s5_mixer(x, mix_norm[0], bw, cw, tab, ssm_d[0], w_glu[0].astype(BF16), rows=128).reshape(t, d)
    w_gate_up = w_gate_up.astype(BF16)
    w_down = w_down.astype(BF16)
    x = _ffn(x, ffn_norm[0], w_gate_up, w_down, 0, tm=512, tf=512)

    w_all = jnp.concatenate([w_q[0], w_kv], axis=1).astype(BF16)
    q_gain = jnp.repeat(q_norm[0].astype(F32) * (HEAD_DIM ** -0.5 * LOG2E), n_heads, axis=0)
    k_gain = jnp.tile(k_norm.astype(F32)[None], (n_heads, 1))
    v_gain = jnp.ones((n_heads, HEAD_DIM), F32)
    hgain = jnp.concatenate([q_gain, k_gain, v_gain], axis=0).reshape(-1, 1, QKV_HEADS_PER_STEP * HEAD_DIM)
    gains = jnp.stack([mix_norm[1], kv_norm]).astype(F32)
    q0, q12, kv32, kv16 = _qkv_proj(x.reshape(bsz, seqlen, d), gains, w_all, hgain, n_heads=n_heads)
    merged = _dilated_attn(q0, q12, kv32, kv16, _alibi_slopes(n_heads))
    x = _out_proj(merged.reshape(t, n_heads * HEAD_DIM), w_o[0].astype(BF16), x, tm=512)
    x = _ffn(x, ffn_norm[1], w_gate_up, w_down, 1, tm=512, tf=512)
    return x.reshape(bsz, seqlen, d)
```

```python
import functools
import math

import jax
import jax.numpy as jnp
from jax import lax
from jax.experimental import pallas as pl
from jax.experimental.pallas import tpu as pltpu

F32 = jnp.float32
BF16 = jnp.bfloat16

EPS = 1e-6
SSM_GROUP = 16
SSM_STATE = 64
HEAD_DIM = 128
ATTN_BLOCK = 128
DILATION_CFG = ((128, 1), (512, 4), (2048, 16))
N_GROUPS = len(DILATION_CFG)
NEG_BIG = -1e30

V7X_SUBLANES = 8
V7X_LANES = 128
V7X_MXU_DIM = 256
V7X_VMEM_BYTES = 64 * 1024 * 1024
VMEM_LIMIT = V7X_VMEM_BYTES - 8 * 1024 * 1024


def _params(semantics):
    return pltpu.CompilerParams(dimension_semantics=semantics, vmem_limit_bytes=VMEM_LIMIT)


def _const_spec(shape):
    nd = len(shape)
    return pl.BlockSpec(shape, lambda *_: (0,) * nd, pipeline_mode=pl.Buffered(1))


def _rms_scale(x):
    return lax.rsqrt(jnp.mean(x * x, axis=-1, keepdims=True) + EPS)


S5_TILE_GROUPS = V7X_MXU_DIM // SSM_GROUP
S5_TILE_STATES = S5_TILE_GROUPS * SSM_STATE
S5_STEPS = (1, 2, 4)
S5_SUPER = V7X_SUBLANES * V7X_SUBLANES
S5_TAB_POW = 0
S5_TAB_HS = V7X_SUBLANES
S5_TAB_CARRY = S5_TAB_HS + len(S5_STEPS)
S5_N_TABS = S5_TAB_CARRY + 1


def _cmul(ar, ai, br, bi):
    return ar * br - ai * bi, ar * bi + ai * br


def _s5_scan_column(slab_ref, tab_ref, m, c, row0, h):
    n_col = slab_ref.shape[0] // 2
    lanes = slice(c * V7X_LANES, (c + 1) * V7X_LANES)

    def tab(t):
        return tab_ref[m, t, 0, :, lanes], tab_ref[m, t, 1, :, lanes]

    def rows(k):
        return pl.ds(row0 + k * V7X_SUBLANES, V7X_SUBLANES)

    lam = tab(S5_TAB_POW)
    er = ei = None
    for k in range(V7X_SUBLANES):
        xr = slab_ref[c, rows(k), :]
        xi = slab_ref[n_col + c, rows(k), :]
        if k:
            pr, pi = _cmul(lam[0], lam[1], er, ei)
            xr, xi = xr + pr, xi + pi
            slab_ref[c, rows(k), :] = xr
            slab_ref[n_col + c, rows(k), :] = xi
        er, ei = xr, xi

    for t, s in enumerate(S5_STEPS):
        ar, ai = tab(S5_TAB_HS + t)
        pr, pi = _cmul(ar, ai, pltpu.roll(er, s, 0), pltpu.roll(ei, s, 0))
        er, ei = er + pr, ei + pi
    cr, ci = tab(S5_TAB_CARRY)
    pr, pi = _cmul(cr, ci, h[0], h[1])
    er, ei = er + pr, ei + pi
    first = lax.broadcasted_iota(jnp.int32, er.shape, 0) == 0
    inr = jnp.where(first, h[0], pltpu.roll(er, 1, 0))
    ini = jnp.where(first, h[1], pltpu.roll(ei, 1, 0))

    for k in range(V7X_SUBLANES):
        qr, qi = tab(S5_TAB_POW + k)
        pr, pi = _cmul(qr, qi, inr, ini)
        slab_ref[c, rows(k), :] = slab_ref[c, rows(k), :] + pr
        slab_ref[n_col + c, rows(k), :] = slab_ref[n_col + c, rows(k), :] + pi

    last = V7X_SUBLANES - 1
    return (jnp.broadcast_to(er[last:, :], er.shape), jnp.broadcast_to(ei[last:, :], ei.shape))


def _tile_transpose(ref, rows):
    return jnp.concatenate([ref[pl.ds(row0 + k, V7X_SUBLANES, stride=V7X_SUBLANES), :]
                            for row0 in range(0, rows, S5_SUPER) for k in range(V7X_SUBLANES)], axis=0)


def _sigmoid(x):
    return 1.0 / (1.0 + jnp.exp(-x))


def _s5_kernel(x_ref, xprev_ref, gain_ref, bw_ref, cw_ref, tab_ref, dskip_ref, wglu_ref, o_ref,
               h_ref, slabs_ref, perm_ref, ubf_ref, yperm_ref, gnew_ref, gold_ref, *, rows):
    n_tiles = bw_ref.shape[0]
    n_slab = slabs_ref.shape[1]
    n_col = n_slab // 2
    d = x_ref.shape[1]

    @pl.when(pl.program_id(1) == 0)
    def _():
        h_ref[...] = jnp.zeros_like(h_ref)
        gold_ref[...] = jnp.zeros_like(gold_ref)

    def glu_columns(j):
        cols = slice(j * V7X_MXU_DIM, (j + 1) * V7X_MXU_DIM)
        g_prev = gold_ref[...]
        val = jnp.dot(g_prev, wglu_ref[:, cols], preferred_element_type=F32)
        gate = jnp.dot(g_prev, wglu_ref[:, d + j * V7X_MXU_DIM:d + (j + 1) * V7X_MXU_DIM],
                       preferred_element_type=F32)
        o_ref[:, cols] = xprev_ref[:, cols] + val * _sigmoid(gate)

    x = x_ref[...]
    u = x * _rms_scale(x) * gain_ref[...]
    for s in range(perm_ref.shape[0]):
        perm_ref[s] = u[:, s * V7X_LANES:(s + 1) * V7X_LANES]
    for s in range(perm_ref.shape[0]):
        ubf_ref[:, s * V7X_LANES:(s + 1) * V7X_LANES] = _tile_transpose(perm_ref.at[s], rows).astype(BF16)

    n_buf = slabs_ref.shape[0]

    def project_in(m):
        slab_ref = slabs_ref.at[m % n_buf]
        bu = jnp.dot(ubf_ref[:, m * V7X_MXU_DIM:(m + 1) * V7X_MXU_DIM], bw_ref[m], preferred_element_type=F32)
        for s in range(n_slab):
            slab_ref[s] = bu[:, s * V7X_LANES:(s + 1) * V7X_LANES]

    def scan_tile(m):
        slab_ref = slabs_ref.at[m % n_buf]
        for c in range(n_col):
            h = (h_ref[m, 0, :, c * V7X_LANES:(c + 1) * V7X_LANES],
                 h_ref[m, 1, :, c * V7X_LANES:(c + 1) * V7X_LANES])
            for row0 in range(0, rows, S5_SUPER):
                h = _s5_scan_column(slab_ref, tab_ref, m, c, row0, h)
            h_ref[m, 0, :, c * V7X_LANES:(c + 1) * V7X_LANES] = h[0]
            h_ref[m, 1, :, c * V7X_LANES:(c + 1) * V7X_LANES] = h[1]

    def project_out(m):
        lo = m * V7X_MXU_DIM
        hi = lo + V7X_MXU_DIM
        slab_ref = slabs_ref.at[m % n_buf]
        st = jnp.concatenate([slab_ref[s] for s in range(n_slab)], axis=1).astype(BF16)
        y = jnp.dot(st, cw_ref[m], preferred_element_type=F32)
        y_ref = yperm_ref.at[m % 2]
        for s in range(y_ref.shape[0]):
            y_ref[s] = y[:, s * V7X_LANES:(s + 1) * V7X_LANES]
        y = jnp.concatenate([_tile_transpose(y_ref.at[s], rows) for s in range(y_ref.shape[0])], axis=1)
        y = y + dskip_ref[:, lo:hi] * u[:, lo:hi]
        gnew_ref[:, lo:hi] = jax.nn.gelu(y).astype(BF16)

    for m in range(n_tiles):
        project_in(m)
        glu_columns(m)
        scan_tile(m)
        project_out(m)

    gold_ref[...] = gnew_ref[...]


def _s5_prepare(a_re, a_im, log_dt, b_re, b_im, c_re, c_im):
    lam_re = jnp.minimum(a_re.astype(F32), -1e-4)
    lam_im = a_im.astype(F32)
    dt = jnp.exp(log_dt.astype(F32))[:, None]
    mag = jnp.exp(lam_re * dt)
    ang = lam_im * dt
    lb_re = mag * jnp.cos(ang)
    lb_im = mag * jnp.sin(ang)
    den = lam_re * lam_re + lam_im * lam_im
    nr = lb_re - 1.0
    ni = lb_im
    coef_re = (nr * lam_re + ni * lam_im) / den
    coef_im = (ni * lam_re - nr * lam_im) / den
    br = b_re.astype(F32)
    bi = b_im.astype(F32)
    bb_re = coef_re[..., None] * br - coef_im[..., None] * bi
    bb_im = coef_re[..., None] * bi + coef_im[..., None] * br

    n_groups = a_re.shape[0]
    n_tiles = n_groups // S5_TILE_GROUPS
    same_group = jnp.eye(S5_TILE_GROUPS, dtype=bool)[None, :, None, :, None]

    def block_diag(w, n_rows, n_cols):
        w = w.astype(BF16).reshape(n_tiles, S5_TILE_GROUPS, w.shape[1], 1, w.shape[2])
        return jnp.where(same_group, w, jnp.zeros((), BF16)).reshape(n_tiles, n_rows, n_cols)

    def pack_b(w):
        return block_diag(w.transpose(0, 2, 1), V7X_MXU_DIM, S5_TILE_STATES)

    def pack_c(w):
        return block_diag(w.transpose(0, 2, 1), S5_TILE_STATES, V7X_MXU_DIM)

    bw = jnp.concatenate([pack_b(bb_re), pack_b(bb_im)], axis=-1)
    cw = jnp.concatenate([pack_c(c_re.astype(F32)), pack_c(-c_im.astype(F32))], axis=1)

    def powers(base, n):
        out = [base]
        for _ in range(n - 1):
            out.append(_cmul(out[-1][0], out[-1][1], base[0], base[1]))
        return jnp.stack([jnp.stack(p) for p in out]).reshape(n, 2, n_tiles, S5_TILE_STATES)

    nsub = V7X_SUBLANES
    pw = powers((lb_re, lb_im), nsub)
    pw8 = powers((pw[nsub - 1, 0].reshape(lb_re.shape), pw[nsub - 1, 1].reshape(lb_re.shape)), nsub)
    sub = jnp.arange(nsub)[None, :, None]
    same = jnp.broadcast_to(pw[:, :, :, None, :], (nsub, 2, n_tiles, nsub, S5_TILE_STATES))
    hs = jnp.stack([jnp.where(sub >= s, pw8[s - 1][:, :, None, :], 0.0) for s in S5_STEPS])
    carry = pw8.transpose(1, 2, 0, 3)[None]
    tab = jnp.concatenate([same, hs, carry], axis=0).transpose(2, 0, 1, 3, 4)
    return bw, cw, tab


def _s5_mixer(x, gain, bw, cw, tab, d_skip, w_glu, *, rows):
    bsz, seqlen, d = x.shape
    n_tiles = bw.shape[0]
    n_blocks = seqlen // rows
    two_ns = 2 * S5_TILE_STATES
    lagged = pl.BlockSpec((None, rows, d), lambda b, c: (b, jnp.maximum(c - 1, 0), 0))
    return pl.pallas_call(
        functools.partial(_s5_kernel, rows=rows),
        grid=(bsz, n_blocks + 1),
        in_specs=[
            pl.BlockSpec((None, rows, d), lambda b, c: (b, jnp.minimum(c, n_blocks - 1), 0)),
            lagged,
            _const_spec((1, d)),
            _const_spec(bw.shape),
            _const_spec(cw.shape),
            _const_spec(tab.shape),
            _const_spec((1, d)),
            _const_spec(w_glu.shape),
        ],
        out_specs=lagged,
        out_shape=jax.ShapeDtypeStruct((bsz, seqlen, d), F32),
        scratch_shapes=[
            pltpu.VMEM((n_tiles, 2, V7X_SUBLANES, S5_TILE_STATES), F32),
            pltpu.VMEM((2, two_ns // V7X_LANES, rows, V7X_LANES), F32),
            pltpu.VMEM((d // V7X_LANES, rows, V7X_LANES), F32),
            pltpu.VMEM((rows, d), BF16),
            pltpu.VMEM((2, V7X_MXU_DIM // V7X_LANES, rows, V7X_LANES), F32),
            pltpu.VMEM((rows, d), BF16),
            pltpu.VMEM((rows, d), BF16),
        ],
        compiler_params=_params(("arbitrary", "arbitrary")),
        name="s5_mixer",
    )(x, x, gain.reshape(1, d), bw, cw, tab, d_skip.reshape(1, d), w_glu)


def _ffn_kernel(x_ref, gain_ref, wg_ref, wu_ref, wd_ref, o_ref, h_ref, acc_ref):
    j = pl.program_id(1)

    @pl.when(j == 0)
    def _():
        x = x_ref[...]
        h_ref[...] = (x * _rms_scale(x) * gain_ref[...]).astype(BF16)
        acc_ref[...] = jnp.zeros_like(acc_ref)

    h = h_ref[...]
    gate = jnp.dot(h, wg_ref[...], preferred_element_type=F32)
    up = jnp.dot(h, wu_ref[...], preferred_element_type=F32)
    act = (gate * _sigmoid(gate) * up).astype(BF16)
    acc_ref[...] += jnp.dot(act, wd_ref[...], preferred_element_type=F32)

    @pl.when(j == pl.num_programs(1) - 1)
    def _():
        o_ref[...] = x_ref[...] + acc_ref[...]


def _ffn(x, gain, w_gate_up, w_down, layer, *, tm, tf):
    t, d = x.shape
    d_ff = w_down.shape[1]
    nj = d_ff // tf
    return pl.pallas_call(
        _ffn_kernel,
        grid=(t // tm, nj),
        in_specs=[
            pl.BlockSpec((tm, d), lambda i, j: (i, 0)),
            _const_spec((1, d)),
            pl.BlockSpec((None, d, tf), lambda i, j: (layer, 0, j)),
            pl.BlockSpec((None, d, tf), lambda i, j: (layer, 0, j + nj)),
            pl.BlockSpec((None, tf, d), lambda i, j: (layer, j, 0)),
        ],
        out_specs=pl.BlockSpec((tm, d), lambda i, j: (i, 0)),
        out_shape=jax.ShapeDtypeStruct((t, d), F32),
        scratch_shapes=[pltpu.VMEM((tm, d), BF16), pltpu.VMEM((tm, d), F32)],
        compiler_params=_params(("arbitrary", "arbitrary")),
        name="ffn",
    )(x, gain.reshape(1, d), w_gate_up, w_gate_up, w_down)


ATTN_SPAN = ATTN_BLOCK * max(d for _, d in DILATION_CFG)
SPAN_RES = ATTN_SPAN // ATTN_BLOCK
QKV_HEADS_PER_STEP = 4
QKV_ROWS = ATTN_SPAN // 2
LOG2E = math.log2(math.e)


def _head_norm(y, gain):
    parts = []
    for a in range(y.shape[1] // HEAD_DIM):
        ya = y[:, a * HEAD_DIM:(a + 1) * HEAD_DIM]
        parts.append(ya * _rms_scale(ya))
    return jnp.concatenate(parts, axis=1) * gain


def _store_heads(o_ref, y):
    for a in range(o_ref.shape[0]):
        ya = y[:, a * HEAD_DIM:(a + 1) * HEAD_DIM]
        o_ref[a] = ya.reshape(o_ref.shape[1:]).astype(o_ref.dtype)


def _qkv_kernel(x_ref, gains_ref, w_ref, hgain_ref, q0_ref, q12_ref, kv32_ref, kv16_ref,
                slab_ref, hq_ref, hk_ref, *, n_q0, n_q, n_k):
    j = pl.program_id(2)

    @pl.when(j == 0)
    def _():
        x = x_ref[...]
        xn = x * _rms_scale(x)
        per_res = x.shape[0] // SPAN_RES
        n_slab = slab_ref.shape[0]
        for c in range(n_slab):
            slab_ref[c] = xn[:, c * V7X_LANES:(c + 1) * V7X_LANES]
        for c in range(n_slab):
            lanes = slice(c * V7X_LANES, (c + 1) * V7X_LANES)
            for r in range(SPAN_RES):
                piece = slab_ref[c, pl.ds(r, per_res, stride=SPAN_RES), :]
                rows = slice(r * per_res, (r + 1) * per_res)
                hq_ref[rows, lanes] = (piece * gains_ref[0:1, lanes]).astype(BF16)
                hk_ref[rows, lanes] = (piece * gains_ref[1:2, lanes]).astype(BF16)

    @pl.when(j < n_q0)
    def _():
        y = jnp.dot(hq_ref[...], w_ref[...], preferred_element_type=F32)
        _store_heads(q0_ref, _head_norm(y, hgain_ref[0]))

    @pl.when((j >= n_q0) & (j < n_q))
    def _():
        y = jnp.dot(hq_ref[...], w_ref[...], preferred_element_type=F32)
        _store_heads(q12_ref, _head_norm(y, hgain_ref[0]))

    @pl.when((j >= n_q) & (j < n_q + n_k))
    def _():
        y = _head_norm(jnp.dot(hk_ref[...], w_ref[...], preferred_element_type=F32), hgain_ref[0])
        _store_heads(kv32_ref, y)
        _store_heads(kv16_ref, y)

    @pl.when(j >= n_q + n_k)
    def _():
        y = jnp.dot(hk_ref[...], w_ref[...], preferred_element_type=F32)
        _store_heads(kv32_ref, y)
        _store_heads(kv16_ref, y)


def _qkv_proj(x, gains, w_all, hgain, *, n_heads):
    bsz, seqlen, d = x.shape
    n_span = seqlen // ATTN_SPAN
    hb = QKV_HEADS_PER_STEP
    tn = hb * HEAD_DIM
    n_q0 = n_heads // hb
    n_q = N_GROUPS * n_heads // hb
    n_k = n_heads // hb
    n_steps = w_all.shape[1] // tn
    rows = QKV_ROWS
    tiles_per_span = ATTN_SPAN // rows
    per_res = rows // SPAN_RES

    def head_block(lo, hi):
        return lambda b, u, j: (b, u // tiles_per_span, jnp.clip(j - lo, 0, hi - lo - 1), 0, u % tiles_per_span, 0)

    def out(n_h, dtype):
        return jax.ShapeDtypeStruct((bsz, n_span, n_h, SPAN_RES, ATTN_BLOCK, HEAD_DIM), dtype)

    blk = (None, None, hb, SPAN_RES, per_res, HEAD_DIM)
    return pl.pallas_call(
        functools.partial(_qkv_kernel, n_q0=n_q0, n_q=n_q, n_k=n_k),
        grid=(bsz, seqlen // rows, n_steps),
        in_specs=[
            pl.BlockSpec((None, rows, d), lambda b, u, j: (b, u, 0)),
            _const_spec((2, d)),
            pl.BlockSpec((d, tn), lambda b, u, j: (0, j)),
            pl.BlockSpec((1, 1, tn), lambda b, u, j: (j, 0, 0)),
        ],
        out_specs=[
            pl.BlockSpec(blk, head_block(0, n_q0)),
            pl.BlockSpec(blk, head_block(n_q0, n_q)),
            pl.BlockSpec(blk, head_block(n_q, n_steps)),
            pl.BlockSpec(blk, head_block(n_q, n_steps)),
        ],
        out_shape=[out(n_heads, F32), out((N_GROUPS - 1) * n_heads, BF16),
                   out(2 * n_heads, F32), out(2 * n_heads, BF16)],
        scratch_shapes=[pltpu.VMEM((d // V7X_LANES, rows, V7X_LANES), F32),
                        pltpu.VMEM((rows, d), BF16), pltpu.VMEM((rows, d), BF16)],
        compiler_params=_params(("arbitrary", "arbitrary", "arbitrary")),
        name="qkv_proj",
    )(x, gains, w_all, hgain)


def _attend(q, kk, vv, bias):
    s = lax.dot_general(q, kk, (((1,), (1,)), ((), ())), preferred_element_type=F32) + bias
    m = jnp.max(s, axis=-1, keepdims=True)
    p = jnp.exp2(s - m)
    l = jnp.sum(p, axis=-1, keepdims=True)
    o = jnp.dot(p.astype(BF16), vv, preferred_element_type=F32) * (1.0 / l)
    return o, m + jnp.log2(l)


def _attn_kernel(slope_ref, q0_ref, q1_ref, q2_ref, k32_ref, v32_ref, k32p_ref, v32p_ref,
                 k16_ref, v16_ref, k16p_ref, v16p_ref, o_ref, og_ref, lg_ref):
    s_idx = pl.program_id(1)
    head = pl.program_id(2)
    blk = ATTN_BLOCK
    row = lax.broadcasted_iota(jnp.int32, (blk, 2 * blk), 0)
    col = lax.broadcasted_iota(jnp.int32, (blk, 2 * blk), 1)
    in_prev = col < blk
    no_prev = jnp.where(in_prev, jnp.where(s_idx == 0, NEG_BIG, 0.0).astype(F32), 0.0)

    sources = ((q0_ref, k32_ref, v32_ref, k32p_ref, v32p_ref),
               (q1_ref, k16_ref, v16_ref, k16p_ref, v16p_ref),
               (q2_ref, k16_ref, v16_ref, k16p_ref, v16p_ref))
    for g, ((window, dil), (q_ref, k_ref, v_ref, kp_ref, vp_ref)) in enumerate(zip(DILATION_CFG, sources)):
        n_pieces = SPAN_RES // dil
        piece_rows = blk // n_pieces
        shift = piece_rows.bit_length() - 1
        prev_lo = kp_ref.shape[1] - piece_rows

        def pos(x, n_pieces=n_pieces, piece_rows=piece_rows, shift=shift):
            return jnp.right_shift(x, shift) + n_pieces * jnp.bitwise_and(x, piece_rows - 1)

        dist = (blk + pos(row)) - (pos(jnp.bitwise_and(col, blk - 1)) + jnp.where(in_prev, 0, blk))
        coef = slope_ref[g, head] * (-dil * LOG2E)
        bias = jnp.where((dist >= 0) & (dist <= window // dil), dist.astype(F32) * coef, NEG_BIG)
        bias_first = bias + no_prev

        for rd in range(dil):
            rs = [rd + dil * j for j in range(n_pieces)]
            for n in range(n_pieces):
                lo = n * piece_rows

                def gather(cur_ref, prev_ref, n=n, lo=lo, rs=rs):
                    cur = [cur_ref[r, lo:lo + piece_rows, :] for r in rs]
                    if n == 0:
                        prev = [prev_ref[r, prev_lo:prev_lo + piece_rows, :] for r in rs]
                    else:
                        prev = [cur_ref[r, lo - piece_rows:lo, :] for r in rs]
                    return jnp.concatenate(prev + cur, axis=0).astype(BF16)

                q = jnp.concatenate([q_ref[r, lo:lo + piece_rows, :] for r in rs], axis=0).astype(BF16)
                o, lse = _attend(q, gather(k_ref, kp_ref), gather(v_ref, vp_ref),
                                 bias_first if n == 0 else bias)
                lse = jnp.broadcast_to(lse, (blk, HEAD_DIM))
                for j, r in enumerate(rs):
                    og_ref[g, r, lo:lo + piece_rows, :] = o[j * piece_rows:(j + 1) * piece_rows]
                    lg_ref[g, r, lo:lo + piece_rows, :] = lse[j * piece_rows:(j + 1) * piece_rows]

    lses = [lg_ref[g] for g in range(N_GROUPS)]
    top = functools.reduce(jnp.maximum, lses)
    wts = [jnp.exp2(l - top) for l in lses]
    num = sum(w * og_ref[g] for g, w in enumerate(wts))
    merged = num / sum(wts)
    for r in range(SPAN_RES):
        o_ref[pl.ds(r, blk, stride=SPAN_RES), :] = merged[r]


def _dilated_attn(q0, q12, kv32, kv16, slopes):
    bsz, n_span, n_heads = q0.shape[:3]
    full = (None, None, None, SPAN_RES, ATTN_BLOCK, HEAD_DIM)
    tail = (None, None, None, SPAN_RES, V7X_SUBLANES, HEAD_DIM)
    last_tail = ATTN_BLOCK // V7X_SUBLANES - 1

    def cur(base):
        return lambda b, s, h: (b, s, base + h, 0, 0, 0)

    def prev(base, row_block=0):
        return lambda b, s, h: (b, jnp.maximum(s - 1, 0), base + h, 0, row_block, 0)

    return pl.pallas_call(
        _attn_kernel,
        grid=(bsz, n_span, n_heads),
        in_specs=[
            pl.BlockSpec(memory_space=pltpu.SMEM),
            pl.BlockSpec(full, cur(0)),
            pl.BlockSpec(full, cur(0)),
            pl.BlockSpec(full, cur(n_heads)),
            pl.BlockSpec(full, cur(0)),
            pl.BlockSpec(full, cur(n_heads)),
            pl.BlockSpec(tail, prev(0, last_tail)),
            pl.BlockSpec(tail, prev(n_heads, last_tail)),
            pl.BlockSpec(full, cur(0)),
            pl.BlockSpec(full, cur(n_heads)),
            pl.BlockSpec(full, prev(0)),
            pl.BlockSpec(full, prev(n_heads)),
        ],
        out_specs=pl.BlockSpec((None, ATTN_SPAN, HEAD_DIM), lambda b, s, h: (b, s, h)),
        out_shape=jax.ShapeDtypeStruct((bsz, n_span * ATTN_SPAN, n_heads * HEAD_DIM), F32),
        scratch_shapes=[
            pltpu.VMEM((N_GROUPS, SPAN_RES, ATTN_BLOCK, HEAD_DIM), F32),
            pltpu.VMEM((N_GROUPS, SPAN_RES, ATTN_BLOCK, HEAD_DIM), F32),
        ],
        compiler_params=_params(("arbitrary", "arbitrary", "arbitrary")),
        name="dilated_attn",
    )(slopes, q0, q12, q12, kv32, kv32, kv32, kv32, kv16, kv16, kv16, kv16)


def _out_kernel(a_ref, w_ref, x_ref, o_ref):
    a = a_ref[...].astype(BF16)
    o_ref[...] = x_ref[...] + jnp.dot(a, w_ref[...], preferred_element_type=F32)


def _out_proj(a, w, x, *, tm):
    t, d = x.shape
    k = a.shape[1]
    return pl.pallas_call(
        _out_kernel,
        grid=(t // tm,),
        in_specs=[
            pl.BlockSpec((tm, k), lambda i: (i, 0)),
            _const_spec((k, d)),
            pl.BlockSpec((tm, d), lambda i: (i, 0)),
        ],
        out_specs=pl.BlockSpec((tm, d), lambda i: (i, 0)),
        out_shape=jax.ShapeDtypeStruct((t, d), F32),
        compiler_params=_params(("arbitrary",)),
        name="out_proj",
    )(a, w, x)


def _alibi_slopes(n_heads):
    n = N_GROUPS * n_heads
    i = jnp.arange(1, n + 1, dtype=F32)
    return jnp.exp2(-8.0 * i / n).reshape(n_heads, N_GROUPS).T


def kernel(x, mix_norm, ffn_norm, ssm_a_re, ssm_a_im, ssm_log_dt, ssm_b_re, ssm_b_im, ssm_c_re, ssm_c_im,
           ssm_d, w_glu, kv_norm, w_kv, k_norm, w_q, q_norm, w_o, w_gate_up, w_down):
    bsz, seqlen, d = x.shape
    t = bsz * seqlen
    n_heads = w_o.shape[1] // HEAD_DIM
    assert ssm_a_re.shape[0] == 1 and w_q.shape[0] == 1, "one S5 layer followed by one attention layer"
    assert seqlen % ATTN_SPAN == 0

    bw, cw, tab = _s5_prepare(ssm_a_re[0], ssm_a_im[0], ssm_log_dt[0], ssm_b_re[0], ssm_b_im[0],
                              ssm_c_re[0], ssm_c_im[0])
    x = _s5_mixer(x, mix_norm[0], bw, cw, tab, ssm_d[0], w_glu[0].astype(BF16), rows=128).reshape(t, d)
    w_gate_up = w_gate_up.astype(BF16)
    w_down = w_down.astype(BF16)
    x = _ffn(x, ffn_norm[0], w_gate_up, w_down, 0, tm=512, tf=512)

    w_all = jnp.concatenate([w_q[0], w_kv], axis=1).astype(BF16)
    q_gain = jnp.repeat(q_norm[0].astype(F32) * (HEAD_DIM ** -0.5 * LOG2E), n_heads, axis=0)
    k_gain = jnp.tile(k_norm.astype(F32)[None], (n_heads, 1))
    v_gain = jnp.ones((n_heads, HEAD_DIM), F32)
    hgain = jnp.concatenate([q_gain, k_gain, v_gain], axis=0).reshape(-1, 1, QKV_HEADS_PER_STEP * HEAD_DIM)
    gains = jnp.stack([mix_norm[1], kv_norm]).astype(F32)
    q0, q12, kv32, kv16 = _qkv_proj(x.reshape(bsz, seqlen, d), gains, w_all, hgain, n_heads=n_heads)
    merged = _dilated_attn(q0, q12, kv32, kv16, _alibi_slopes(n_heads))
    x = _out_proj(merged.reshape(t, n_heads * HEAD_DIM), w_o[0].astype(BF16), x, tm=512)
    x = _ffn(x, ffn_norm[1], w_gate_up, w_down, 1, tm=512, tf=512)
    return x.reshape(bsz, seqlen, d)
```

```python
import functools
import math

import jax
import jax.numpy as jnp
from jax import lax
from jax.experimental import pallas as pl
from jax.experimental.pallas import tpu as pltpu

F32 = jnp.float32
BF16 = jnp.bfloat16

EPS = 1e-6
SSM_GROUP = 16
SSM_STATE = 64
HEAD_DIM = 128
ATTN_BLOCK = 128
DILATION_CFG = ((128, 1), (512, 4), (2048, 16))
N_GROUPS = len(DILATION_CFG)
NEG_BIG = -1e30

V7X_SUBLANES = 8
V7X_LANES = 128
V7X_MXU_DIM = 256
V7X_VMEM_BYTES = 64 * 1024 * 1024
VMEM_LIMIT = V7X_VMEM_BYTES - 8 * 1024 * 1024


def _params(semantics):
    return pltpu.CompilerParams(dimension_semantics=semantics, vmem_limit_bytes=VMEM_LIMIT)


def _const_spec(shape):
    nd = len(shape)
    return pl.BlockSpec(shape, lambda *_: (0,) * nd, pipeline_mode=pl.Buffered(1))


def _rms_scale(x):
    return lax.rsqrt(jnp.mean(x * x, axis=-1, keepdims=True) + EPS)


S5_TILE_GROUPS = V7X_MXU_DIM // SSM_GROUP
S5_TILE_STATES = S5_TILE_GROUPS * SSM_STATE
S5_STEPS = (1, 2, 4)
S5_SUPER = V7X_SUBLANES * V7X_SUBLANES
S5_TAB_POW = 0
S5_TAB_HS = V7X_SUBLANES
S5_TAB_CARRY = S5_TAB_HS + len(S5_STEPS)
S5_N_TABS = S5_TAB_CARRY + 1


def _cmul(ar, ai, br, bi):
    return ar * br - ai * bi, ar * bi + ai * br


def _s5_scan_column(slab_ref, tab_ref, m, c, row0, h):
    n_col = slab_ref.shape[0] // 2
    lanes = slice(c * V7X_LANES, (c + 1) * V7X_LANES)

    def tab(t):
        return tab_ref[m, t, 0, :, lanes], tab_ref[m, t, 1, :, lanes]

    def rows(k):
        return pl.ds(row0 + k * V7X_SUBLANES, V7X_SUBLANES)

    lam = tab(S5_TAB_POW)
    er = ei = None
    for k in range(V7X_SUBLANES):
        xr = slab_ref[c, rows(k), :]
        xi = slab_ref[n_col + c, rows(k), :]
        if k:
            pr, pi = _cmul(lam[0], lam[1], er, ei)
            xr, xi = xr + pr, xi + pi
            slab_ref[c, rows(k), :] = xr
            slab_ref[n_col + c, rows(k), :] = xi
        er, ei = xr, xi

    for t, s in enumerate(S5_STEPS):
        ar, ai = tab(S5_TAB_HS + t)
        pr, pi = _cmul(ar, ai, pltpu.roll(er, s, 0), pltpu.roll(ei, s, 0))
        er, ei = er + pr, ei + pi
    cr, ci = tab(S5_TAB_CARRY)
    pr, pi = _cmul(cr, ci, h[0], h[1])
    er, ei = er + pr, ei + pi
    first = lax.broadcasted_iota(jnp.int32, er.shape, 0) == 0
    inr = jnp.where(first, h[0], pltpu.roll(er, 1, 0))
    ini = jnp.where(first, h[1], pltpu.roll(ei, 1, 0))

    for k in range(V7X_SUBLANES):
        qr, qi = tab(S5_TAB_POW + k)
        pr, pi = _cmul(qr, qi, inr, ini)
        slab_ref[c, rows(k), :] = slab_ref[c, rows(k), :] + pr
        slab_ref[n_col + c, rows(k), :] = slab_ref[n_col + c, rows(k), :] + pi

    last = V7X_SUBLANES - 1
    return (jnp.broadcast_to(er[last:, :], er.shape), jnp.broadcast_to(ei[last:, :], ei.shape))


def _tile_transpose(ref, rows):
    return jnp.concatenate([ref[pl.ds(row0 + k, V7X_SUBLANES, stride=V7X_SUBLANES), :]
                            for row0 in range(0, rows, S5_SUPER) for k in range(V7X_SUBLANES)], axis=0)


def _sigmoid(x):
    return 1.0 / (1.0 + jnp.exp(-x))


def _s5_kernel(x_ref, xprev_ref, gain_ref, bw_ref, cw_ref, tab_ref, dskip_ref, wglu_ref, o_ref,
               h_ref, slabs_ref, perm_ref, ubf_ref, yperm_ref, gnew_ref, gold_ref, *, rows):
    n_tiles = bw_ref.shape[0]
    n_slab = slabs_ref.shape[1]
    n_col = n_slab // 2
    d = x_ref.shape[1]

    @pl.when(pl.program_id(1) == 0)
    def _():
        h_ref[...] = jnp.zeros_like(h_ref)
        gold_ref[...] = jnp.zeros_like(gold_ref)

    def glu_columns(j):
        cols = slice(j * V7X_MXU_DIM, (j + 1) * V7X_MXU_DIM)
        g_prev = gold_ref[...]
        val = jnp.dot(g_prev, wglu_ref[:, cols], preferred_element_type=F32)
        gate = jnp.dot(g_prev, wglu_ref[:, d + j * V7X_MXU_DIM:d + (j + 1) * V7X_MXU_DIM],
                       preferred_element_type=F32)
        o_ref[:, cols] = xprev_ref[:, cols] + val * _sigmoid(gate)

    x = x_ref[...]
    u = x * _rms_scale(x) * gain_ref[...]
    for s in range(perm_ref.shape[0]):
        perm_ref[s] = u[:, s * V7X_LANES:(s + 1) * V7X_LANES]
    for s in range(perm_ref.shape[0]):
        ubf_ref[:, s * V7X_LANES:(s + 1) * V7X_LANES] = _tile_transpose(perm_ref.at[s], rows).astype(BF16)

    n_buf = slabs_ref.shape[0]

    def project_in(m):
        slab_ref = slabs_ref.at[m % n_buf]
        bu = jnp.dot(ubf_ref[:, m * V7X_MXU_DIM:(m + 1) * V7X_MXU_DIM], bw_ref[m], preferred_element_type=F32)
        for s in range(n_slab):
            slab_ref[s] = bu[:, s * V7X_LANES:(s + 1) * V7X_LANES]

    def scan_tile(m):
        slab_ref = slabs_ref.at[m % n_buf]
        for c in range(n_col):
            h = (h_ref[m, 0, :, c * V7X_LANES:(c + 1) * V7X_LANES],
                 h_ref[m, 1, :, c * V7X_LANES:(c + 1) * V7X_LANES])
            for row0 in range(0, rows, S5_SUPER):
                h = _s5_scan_column(slab_ref, tab_ref, m, c, row0, h)
            h_ref[m, 0, :, c * V7X_LANES:(c + 1) * V7X_LANES] = h[0]
            h_ref[m, 1, :, c * V7X_LANES:(c + 1) * V7X_LANES] = h[1]

    def project_out(m):
        lo = m * V7X_MXU_DIM
        hi = lo + V7X_MXU_DIM
        slab_ref = slabs_ref.at[m % n_buf]
        st = jnp.concatenate([slab_ref[s] for s in range(n_slab)], axis=1).astype(BF16)
        y = jnp.dot(st, cw_ref[m], preferred_element_type=F32)
        y_ref = yperm_ref.at[m % 2]
        for s in range(y_ref.shape[0]):
            y_ref[s] = y[:, s * V7X_LANES:(s + 1) * V7X_LANES]
        y = jnp.concatenate([_tile_transpose(y_ref.at[s], rows) for s in range(y_ref.shape[0])], axis=1)
        y = y + dskip_ref[:, lo:hi] * u[:, lo:hi]
        gnew_ref[:, lo:hi] = jax.nn.gelu(y).astype(BF16)

    for m in range(n_tiles):
        project_in(m)
        glu_columns(m)
        scan_tile(m)
        project_out(m)

    gold_ref[...] = gnew_ref[...]


def _s5_prepare(a_re, a_im, log_dt, b_re, b_im, c_re, c_im):
    lam_re = jnp.minimum(a_re.astype(F32), -1e-4)
    lam_im = a_im.astype(F32)
    dt = jnp.exp(log_dt.astype(F32))[:, None]
    mag = jnp.exp(lam_re * dt)
    ang = lam_im * dt
    lb_re = mag * jnp.cos(ang)
    lb_im = mag * jnp.sin(ang)
    den = lam_re * lam_re + lam_im * lam_im
    nr = lb_re - 1.0
    ni = lb_im
    coef_re = (nr * lam_re + ni * lam_im) / den
    coef_im = (ni * lam_re - nr * lam_im) / den
    br = b_re.astype(F32)
    bi = b_im.astype(F32)
    bb_re = coef_re[..., None] * br - coef_im[..., None] * bi
    bb_im = coef_re[..., None] * bi + coef_im[..., None] * br

    n_groups = a_re.shape[0]
    n_tiles = n_groups // S5_TILE_GROUPS
    same_group = jnp.eye(S5_TILE_GROUPS, dtype=bool)[None, :, None, :, None]

    def block_diag(w, n_rows, n_cols):
        w = w.astype(BF16).reshape(n_tiles, S5_TILE_GROUPS, w.shape[1], 1, w.shape[2])
        return jnp.where(same_group, w, jnp.zeros((), BF16)).reshape(n_tiles, n_rows, n_cols)

    def pack_b(w):
        return block_diag(w.transpose(0, 2, 1), V7X_MXU_DIM, S5_TILE_STATES)

    def pack_c(w):
        return block_diag(w.transpose(0, 2, 1), S5_TILE_STATES, V7X_MXU_DIM)

    bw = jnp.concatenate([pack_b(bb_re), pack_b(bb_im)], axis=-1)
    cw = jnp.concatenate([pack_c(c_re.astype(F32)), pack_c(-c_im.astype(F32))], axis=1)

    def powers(base, n):
        out = [base]
        for _ in range(n - 1):
            out.append(_cmul(out[-1][0], out[-1][1], base[0], base[1]))
        return jnp.stack([jnp.stack(p) for p in out]).reshape(n, 2, n_tiles, S5_TILE_STATES)

    nsub = V7X_SUBLANES
    pw = powers((lb_re, lb_im), nsub)
    pw8 = powers((pw[nsub - 1, 0].reshape(lb_re.shape), pw[nsub - 1, 1].reshape(lb_re.shape)), nsub)
    sub = jnp.arange(nsub)[None, :, None]
    same = jnp.broadcast_to(pw[:, :, :, None, :], (nsub, 2, n_tiles, nsub, S5_TILE_STATES))
    hs = jnp.stack([jnp.where(sub >= s, pw8[s - 1][:, :, None, :], 0.0) for s in S5_STEPS])
    carry = pw8.transpose(1, 2, 0, 3)[None]
    tab = jnp.concatenate([same, hs, carry], axis=0).transpose(2, 0, 1, 3, 4)
    return bw, cw, tab


def _s5_mixer(x, gain, bw, cw, tab, d_skip, w_glu, *, rows):
    bsz, seqlen, d = x.shape
    n_tiles = bw.shape[0]
    n_blocks = seqlen // rows
    two_ns = 2 * S5_TILE_STATES
    lagged = pl.BlockSpec((None, rows, d), lambda b, c: (b, jnp.maximum(c - 1, 0), 0))
    return pl.pallas_call(
        functools.partial(_s5_kernel, rows=rows),
        grid=(bsz, n_blocks + 1),
        in_specs=[
            pl.BlockSpec((None, rows, d), lambda b, c: (b, jnp.minimum(c, n_blocks - 1), 0)),
            lagged,
            _const_spec((1, d)),
            _const_spec(bw.shape),
            _const_spec(cw.shape),
            _const_spec(tab.shape),
            _const_spec((1, d)),
            _const_spec(w_glu.shape),
        ],
        out_specs=lagged,
        out_shape=jax.ShapeDtypeStruct((bsz, seqlen, d), F32),
        scratch_shapes=[
            pltpu.VMEM((n_tiles, 2, V7X_SUBLANES, S5_TILE_STATES), F32),
            pltpu.VMEM((2, two_ns // V7X_LANES, rows, V7X_LANES), F32),
            pltpu.VMEM((d // V7X_LANES, rows, V7X_LANES), F32),
            pltpu.VMEM((rows, d), BF16),
            pltpu.VMEM((2, V7X_MXU_DIM // V7X_LANES, rows, V7X_LANES), F32),
            pltpu.VMEM((rows, d), BF16),
            pltpu.VMEM((rows, d), BF16),
        ],
        compiler_params=_params(("arbitrary", "arbitrary")),
        name="s5_mixer",
    )(x, x, gain.reshape(1, d), bw, cw, tab, d_skip.reshape(1, d), w_glu)


def _ffn_kernel(x_ref, gain_ref, wg_ref, wu_ref, wd_ref, o_ref, h_ref, acc_ref):
    j = pl.program_id(1)

    @pl.when(j == 0)
    def _():
        x = x_ref[...]
        h_ref[...] = (x * _rms_scale(x) * gain_ref[...]).astype(BF16)
        acc_ref[...] = jnp.zeros_like(acc_ref)

    h = h_ref[...]
    gate = jnp.dot(h, wg_ref[...], preferred_element_type=F32)
    up = jnp.dot(h, wu_ref[...], preferred_element_type=F32)
    act = (gate * _sigmoid(gate) * up).astype(BF16)
    acc_ref[...] += jnp.dot(act, wd_ref[...], preferred_element_type=F32)

    @pl.when(j == pl.num_programs(1) - 1)
    def _():
        o_ref[...] = x_ref[...] + acc_ref[...]


def _ffn(x, gain, w_gate_up, w_down, layer, *, tm, tf):
    t, d = x.shape
    d_ff = w_down.shape[1]
    nj = d_ff // tf
    return pl.pallas_call(
        _ffn_kernel,
        grid=(t // tm, nj),
        in_specs=[
            pl.BlockSpec((tm, d), lambda i, j: (i, 0)),
            _const_spec((1, d)),
            pl.BlockSpec((None, d, tf), lambda i, j: (layer, 0, j)),
            pl.BlockSpec((None, d, tf), lambda i, j: (layer, 0, j + nj)),
            pl.BlockSpec((None, tf, d), lambda i, j: (layer, j, 0)),
        ],
        out_specs=pl.BlockSpec((tm, d), lambda i, j: (i, 0)),
        out_shape=jax.ShapeDtypeStruct((t, d), F32),
        scratch_shapes=[pltpu.VMEM((tm, d), BF16), pltpu.VMEM((tm, d), F32)],
        compiler_params=_params(("arbitrary", "arbitrary")),
        name="ffn",
    )(x, gain.reshape(1, d), w_gate_up, w_gate_up, w_down)


ATTN_SPAN = ATTN_BLOCK * max(d for _, d in DILATION_CFG)
SPAN_RES = ATTN_SPAN // ATTN_BLOCK
QKV_HEADS_PER_STEP = 4
QKV_ROWS = ATTN_SPAN // 2
LOG2E = math.log2(math.e)


def _head_norm(y, gain):
    parts = []
    for a in range(y.shape[1] // HEAD_DIM):
        ya = y[:, a * HEAD_DIM:(a + 1) * HEAD_DIM]
        parts.append(ya * _rms_scale(ya))
    return jnp.concatenate(parts, axis=1) * gain


def _store_heads(o_ref, y):
    for a in range(o_ref.shape[0]):
        ya = y[:, a * HEAD_DIM:(a + 1) * HEAD_DIM]
        o_ref[a] = ya.reshape(o_ref.shape[1:]).astype(o_ref.dtype)


def _qkv_kernel(x_ref, gains_ref, w_ref, hgain_ref, o32_ref, o16_ref,
                slab_ref, hq_ref, hk_ref, *, n_q0, n_q, n_k):
    j = pl.program_id(2)

    @pl.when(j == 0)
    def _():
        x = x_ref[...]
        xn = x * _rms_scale(x)
        per_res = x.shape[0] // SPAN_RES
        n_slab = slab_ref.shape[0]
        for c in range(n_slab):
            slab_ref[c] = xn[:, c * V7X_LANES:(c + 1) * V7X_LANES]
        for c in range(n_slab):
            lanes = slice(c * V7X_LANES, (c + 1) * V7X_LANES)
            for r in range(SPAN_RES):
                piece = slab_ref[c, pl.ds(r, per_res, stride=SPAN_RES), :]
                rows = slice(r * per_res, (r + 1) * per_res)
                hq_ref[rows, lanes] = (piece * gains_ref[0:1, lanes]).astype(BF16)
                hk_ref[rows, lanes] = (piece * gains_ref[1:2, lanes]).astype(BF16)

    @pl.when(j < n_q0)
    def _():
        y = jnp.dot(hq_ref[...], w_ref[...], preferred_element_type=F32)
        _store_heads(o32_ref, _head_norm(y, hgain_ref[0]))

    @pl.when((j >= n_q0) & (j < n_q))
    def _():
        y = jnp.dot(hq_ref[...], w_ref[...], preferred_element_type=F32)
        _store_heads(o16_ref, _head_norm(y, hgain_ref[0]))

    @pl.when((j >= n_q) & (j < n_q + n_k))
    def _():
        y = _head_norm(jnp.dot(hk_ref[...], w_ref[...], preferred_element_type=F32), hgain_ref[0])
        _store_heads(o32_ref, y)
        _store_heads(o16_ref, y)

    @pl.when(j >= n_q + n_k)
    def _():
        y = jnp.dot(hk_ref[...], w_ref[...], preferred_element_type=F32)
        _store_heads(o32_ref, y)
        _store_heads(o16_ref, y)


def _qkv_proj(x, gains, w_all, hgain, *, n_heads):
    bsz, seqlen, d = x.shape
    n_span = seqlen // ATTN_SPAN
    hb = QKV_HEADS_PER_STEP
    tn = hb * HEAD_DIM
    n_q0 = n_heads // hb
    n_q = N_GROUPS * n_heads // hb
    n_k = n_heads // hb
    n_steps = w_all.shape[1] // tn
    rows = QKV_ROWS
    tiles_per_span = ATTN_SPAN // rows
    per_res = rows // SPAN_RES

    def f32_block(b, u, j):
        blk_idx = jnp.where(j < n_q, jnp.minimum(j, n_q0 - 1), j - (n_q - n_q0))
        return (b, u // tiles_per_span, blk_idx, 0, u % tiles_per_span, 0)

    def bf16_block(b, u, j):
        return (b, u // tiles_per_span, jnp.maximum(j - n_q0, 0), 0, u % tiles_per_span, 0)

    def out(n_h, dtype):
        return jax.ShapeDtypeStruct((bsz, n_span, n_h, SPAN_RES, ATTN_BLOCK, HEAD_DIM), dtype)

    blk = (None, None, hb, SPAN_RES, per_res, HEAD_DIM)
    return pl.pallas_call(
        functools.partial(_qkv_kernel, n_q0=n_q0, n_q=n_q, n_k=n_k),
        grid=(bsz, seqlen // rows, n_steps),
        in_specs=[
            pl.BlockSpec((None, rows, d), lambda b, u, j: (b, u, 0)),
            _const_spec((2, d)),
            pl.BlockSpec((d, tn), lambda b, u, j: (0, j)),
            pl.BlockSpec((1, 1, tn), lambda b, u, j: (j, 0, 0)),
        ],
        out_specs=[pl.BlockSpec(blk, f32_block), pl.BlockSpec(blk, bf16_block)],
        out_shape=[out(3 * n_heads, F32), out((N_GROUPS + 1) * n_heads, BF16)],
        scratch_shapes=[pltpu.VMEM((d // V7X_LANES, rows, V7X_LANES), F32),
                        pltpu.VMEM((rows, d), BF16), pltpu.VMEM((rows, d), BF16)],
        compiler_params=_params(("arbitrary", "arbitrary", "arbitrary")),
        name="qkv_proj",
    )(x, gains, w_all, hgain)


def _attend(q, kk, vv, bias):
    s = lax.dot_general(q, kk, (((1,), (1,)), ((), ())), preferred_element_type=F32) + bias
    m = jnp.max(s, axis=-1, keepdims=True)
    p = jnp.exp2(s - m)
    l = jnp.sum(p, axis=-1, keepdims=True)
    o = jnp.dot(p.astype(BF16), vv, preferred_element_type=F32) * (1.0 / l)
    return o, m + jnp.log2(l)


def _attn_kernel(slope_ref, q0_ref, q1_ref, q2_ref, k32_ref, v32_ref, k32p_ref, v32p_ref,
                 k16_ref, v16_ref, k16p_ref, v16p_ref, o_ref, og_ref, lg_ref):
    s_idx = pl.program_id(1)
    head = pl.program_id(2)
    blk = ATTN_BLOCK
    row = lax.broadcasted_iota(jnp.int32, (blk, 2 * blk), 0)
    col = lax.broadcasted_iota(jnp.int32, (blk, 2 * blk), 1)
    in_prev = col < blk
    no_prev = jnp.where(in_prev, jnp.where(s_idx == 0, NEG_BIG, 0.0).astype(F32), 0.0)

    sources = ((q0_ref, k32_ref, v32_ref, k32p_ref, v32p_ref),
               (q1_ref, k16_ref, v16_ref, k16p_ref, v16p_ref),
               (q2_ref, k16_ref, v16_ref, k16p_ref, v16p_ref))
    for g, ((window, dil), (q_ref, k_ref, v_ref, kp_ref, vp_ref)) in enumerate(zip(DILATION_CFG, sources)):
        n_pieces = SPAN_RES // dil
        piece_rows = blk // n_pieces
        shift = piece_rows.bit_length() - 1
        prev_lo = kp_ref.shape[1] - piece_rows

        def pos(x, n_pieces=n_pieces, piece_rows=piece_rows, shift=shift):
            return jnp.right_shift(x, shift) + n_pieces * jnp.bitwise_and(x, piece_rows - 1)

        dist = (blk + pos(row)) - (pos(jnp.bitwise_and(col, blk - 1)) + jnp.where(in_prev, 0, blk))
        coef = slope_ref[g, head] * (-dil * LOG2E)
        bias = jnp.where((dist >= 0) & (dist <= window // dil), dist.astype(F32) * coef, NEG_BIG)
        bias_first = bias + no_prev

        for rd in range(dil):
            rs = [rd + dil * j for j in range(n_pieces)]
            for n in range(n_pieces):
                lo = n * piece_rows

                def gather(cur_ref, prev_ref, n=n, lo=lo, rs=rs):
                    cur = [cur_ref[r, lo:lo + piece_rows, :] for r in rs]
                    if n == 0:
                        prev = [prev_ref[r, prev_lo:prev_lo + piece_rows, :] for r in rs]
                    else:
                        prev = [cur_ref[r, lo - piece_rows:lo, :] for r in rs]
                    return jnp.concatenate(prev + cur, axis=0).astype(BF16)

                q = jnp.concatenate([q_ref[r, lo:lo + piece_rows, :] for r in rs], axis=0).astype(BF16)
                o, lse = _attend(q, gather(k_ref, kp_ref), gather(v_ref, vp_ref),
                                 bias_first if n == 0 else bias)
                lse = jnp.broadcast_to(lse, (blk, HEAD_DIM))
                for j, r in enumerate(rs):
                    og_ref[g, r, lo:lo + piece_rows, :] = o[j * piece_rows:(j + 1) * piece_rows]
                    lg_ref[g, r, lo:lo + piece_rows, :] = lse[j * piece_rows:(j + 1) * piece_rows]

    lses = [lg_ref[g] for g in range(N_GROUPS)]
    top = functools.reduce(jnp.maximum, lses)
    wts = [jnp.exp2(l - top) for l in lses]
    num = sum(w * og_ref[g] for g, w in enumerate(wts))
    merged = num / sum(wts)
    for r in range(SPAN_RES):
        o_ref[pl.ds(r, blk, stride=SPAN_RES), :] = merged[r]


def _dilated_attn(qkv32, qkv16, slopes, *, n_heads):
    bsz, n_span = qkv32.shape[:2]
    full = (None, None, None, SPAN_RES, ATTN_BLOCK, HEAD_DIM)
    tail = (None, None, None, SPAN_RES, V7X_SUBLANES, HEAD_DIM)
    last_tail = ATTN_BLOCK // V7X_SUBLANES - 1

    def cur(base):
        return lambda b, s, h: (b, s, base + h, 0, 0, 0)

    def prev(base, row_block=0):
        return lambda b, s, h: (b, jnp.maximum(s - 1, 0), base + h, 0, row_block, 0)

    return pl.pallas_call(
        _attn_kernel,
        grid=(bsz, n_span, n_heads),
        in_specs=[
            pl.BlockSpec(memory_space=pltpu.SMEM),
            pl.BlockSpec(full, cur(0)),
            pl.BlockSpec(full, cur(0)),
            pl.BlockSpec(full, cur(n_heads)),
            pl.BlockSpec(full, cur(n_heads)),
            pl.BlockSpec(full, cur(2 * n_heads)),
            pl.BlockSpec(tail, prev(n_heads, last_tail)),
            pl.BlockSpec(tail, prev(2 * n_heads, last_tail)),
            pl.BlockSpec(full, cur(2 * n_heads)),
            pl.BlockSpec(full, cur(3 * n_heads)),
            pl.BlockSpec(full, prev(2 * n_heads)),
            pl.BlockSpec(full, prev(3 * n_heads)),
        ],
        out_specs=pl.BlockSpec((None, ATTN_SPAN, HEAD_DIM), lambda b, s, h: (b, s, h)),
        out_shape=jax.ShapeDtypeStruct((bsz, n_span * ATTN_SPAN, n_heads * HEAD_DIM), F32),
        scratch_shapes=[
            pltpu.VMEM((N_GROUPS, SPAN_RES, ATTN_BLOCK, HEAD_DIM), F32),
            pltpu.VMEM((N_GROUPS, SPAN_RES, ATTN_BLOCK, HEAD_DIM), F32),
        ],
        compiler_params=_params(("arbitrary", "arbitrary", "arbitrary")),
        name="dilated_attn",
    )(slopes, qkv32, qkv16, qkv16, qkv32, qkv32, qkv32, qkv32, qkv16, qkv16, qkv16, qkv16)


def _out_kernel(a_ref, w_ref, x_ref, o_ref):
    a = a_ref[...].astype(BF16)
    o_ref[...] = x_ref[...] + jnp.dot(a, w_ref[...], preferred_element_type=F32)


def _out_proj(a, w, x, *, tm):
    t, d = x.shape
    k = a.shape[1]
    return pl.pallas_call(
        _out_kernel,
        grid=(t // tm,),
        in_specs=[
            pl.BlockSpec((tm, k), lambda i: (i, 0)),
            _const_spec((k, d)),
            pl.BlockSpec((tm, d), lambda i: (i, 0)),
        ],
        out_specs=pl.BlockSpec((tm, d), lambda i: (i, 0)),
        out_shape=jax.ShapeDtypeStruct((t, d), F32),
        compiler_params=_params(("arbitrary",)),
        name="out_proj",
    )(a, w, x)


def _alibi_slopes(n_heads):
    n = N_GROUPS * n_heads
    i = jnp.arange(1, n + 1, dtype=F32)
    return jnp.exp2(-8.0 * i / n).reshape(n_heads, N_GROUPS).T


S5_ROWS = 128
FFN_ROWS = 512
FFN_COLS = 512
OUT_ROWS = 512


def kernel(x, mix_norm, ffn_norm, ssm_a_re, ssm_a_im, ssm_log_dt, ssm_b_re, ssm_b_im, ssm_c_re, ssm_c_im,
           ssm_d, w_glu, kv_norm, w_kv, k_norm, w_q, q_norm, w_o, w_gate_up, w_down):
    bsz, seqlen, d = x.shape
    t = bsz * seqlen
    n_heads = w_o.shape[1] // HEAD_DIM
    assert ssm_a_re.shape[0] == 1 and w_q.shape[0] == 1, "one S5 layer followed by one attention layer"
    assert seqlen % ATTN_SPAN == 0

    bw, cw, tab = _s5_prepare(ssm_a_re[0], ssm_a_im[0], ssm_log_dt[0], ssm_b_re[0], ssm_b_im[0],
                              ssm_c_re[0], ssm_c_im[0])
    x = _s5_mixer(x, mix_norm[0], bw, cw, tab, ssm_d[0], w_glu[0].astype(BF16), rows=S5_ROWS).reshape(t, d)
    w_gate_up = w_gate_up.astype(BF16)
    w_down = w_down.astype(BF16)
    x = _ffn(x, ffn_norm[0], w_gate_up, w_down, 0, tm=FFN_ROWS, tf=FFN_COLS)

    w_all = jnp.concatenate([w_q[0], w_kv], axis=1).astype(BF16)
    q_gain = jnp.repeat(q_norm[0].astype(F32) * (HEAD_DIM ** -0.5 * LOG2E), n_heads, axis=0)
    k_gain = jnp.tile(k_norm.astype(F32)[None], (n_heads, 1))
    v_gain = jnp.ones((n_heads, HEAD_DIM), F32)
    hgain = jnp.concatenate([q_gain, k_gain, v_gain], axis=0).reshape(-1, 1, QKV_HEADS_PER_STEP * HEAD_DIM)
    gains = jnp.stack([mix_norm[1], kv_norm]).astype(F32)
    qkv32, qkv16 = _qkv_proj(x.reshape(bsz, seqlen, d), gains, w_all, hgain, n_heads=n_heads)
    merged = _dilated_attn(qkv32, qkv16, _alibi_slopes(n_heads), n_heads=n_heads)
    x = _out_proj(merged.reshape(t, n_heads * HEAD_DIM), w_o[0].astype(BF16), x, tm=OUT_ROWS)
    x = _ffn(x, ffn_norm[1], w_gate_up, w_down, 1, tm=FFN_ROWS, tf=FFN_COLS)
    return x.reshape(bsz, seqlen, d)
```
